```python
import jax
import jax.numpy as jnp
from jax import lax
import numpy as np

D_MODEL = 1024
BATCH = 4
SEQ = 4096
DEPTH = 4
DEC_BATCH = 128
DEC_SEQ = 4
PAST_LEN = 8192
PAGE_SIZE = 128

N_MIXERS = 3
Q_BLOCK = 128
NORM_EPS = 1e-6

FOX_HEADS = 16
FOX_KV_HEADS = 4
FOX_GROUP = FOX_HEADS // FOX_KV_HEADS
FOX_HEAD_DIM = D_MODEL // FOX_HEADS
FOX_WIDTH = FOX_HEADS * FOX_HEAD_DIM
FOX_KV_WIDTH = FOX_KV_HEADS * FOX_HEAD_DIM
FOX_IN = 2 * FOX_WIDTH + 2 * FOX_KV_WIDTH + FOX_HEADS
FOX_SCALE = FOX_HEAD_DIM ** -0.5
FORGET_BIAS_INIT = 4.0

CONV_CHANNELS = D_MODEL
CONV_TAPS = 31

MLA_HEADS = 16
MLA_NOPE = 64
MLA_ROPE = 32
MLA_V = 64
MLA_Q_LORA = 768
MLA_KV_LORA = 256
MLA_WIDTH = MLA_HEADS * MLA_V
MLA_IN = MLA_Q_LORA + MLA_KV_LORA + MLA_ROPE + MLA_WIDTH
MLA_SCALE = (MLA_NOPE + MLA_ROPE) ** -0.5
ROPE_THETA = 10000.0

kernel_name = 'hybrid_fox_conformer_mla_step'


def _rms_norm(x, g):
    xf = x.astype(jnp.float32)
    y = xf * lax.rsqrt(jnp.mean(xf * xf, axis=-1, keepdims=True) + NORM_EPS)
    return (y * g.astype(jnp.float32)).astype(x.dtype)


def _layer_norm(x, g, b):
    xf = x.astype(jnp.float32)
    xc = xf - jnp.mean(xf, axis=-1, keepdims=True)
    var = jnp.mean(xc * xc, axis=-1, keepdims=True)
    y = xc * lax.rsqrt(var + NORM_EPS) * g.astype(jnp.float32) + b.astype(jnp.float32)
    return y.astype(x.dtype)


def _rope(x, pos):
    half = x.shape[-1] // 2
    inv = ROPE_THETA ** (-jnp.arange(half, dtype=jnp.float32) / half)
    ang = pos.astype(jnp.float32)[:, None] * inv[None, :]
    cos = jnp.cos(ang)[None, :, None, :]
    sin = jnp.sin(ang)[None, :, None, :]
    xf = x.astype(jnp.float32)
    x1, x2 = xf[..., :half], xf[..., half:]
    return jnp.concatenate([x1 * cos - x2 * sin, x1 * sin + x2 * cos], axis=-1).astype(x.dtype)


def _gather_pages(pool, page_table):
    g = pool[page_table]
    return g.reshape((g.shape[0], g.shape[1] * g.shape[2]) + g.shape[3:])


def _gated_out(o, z, w_out):
    return (o * jax.nn.silu(z)) @ w_out


def _block_causal_attention(q, k, v, scale, cum):
    S = q.shape[1]
    outs = []
    for start in range(0, S, Q_BLOCK):
        stop = start + Q_BLOCK
        s = jnp.einsum('bqkgd,bskd->bkgqs', q[:, start:stop], k[:, :stop]).astype(jnp.float32) * scale
        if cum is not None:
            cq = jnp.moveaxis(cum[:, start:stop], 1, -1)[..., :, None]
            ck = jnp.moveaxis(cum[:, :stop], 1, -1)[..., None, :]
            s = s + (cq - ck)
        mask = (start + jnp.arange(Q_BLOCK))[:, None] >= jnp.arange(stop)[None, :]
        p = jax.nn.softmax(jnp.where(mask, s, -jnp.inf), axis=-1).astype(v.dtype)
        outs.append(jnp.einsum('bkgqs,bskd->bqkgd', p, v[:, :stop]))
    return jnp.concatenate(outs, axis=1)


def _fox_project(h, w_in, b_f):
    B, T = h.shape[:2]
    p = h @ w_in
    o0 = FOX_WIDTH
    o1 = o0 + FOX_KV_WIDTH
    o2 = o1 + FOX_KV_WIDTH
    o3 = o2 + FOX_WIDTH
    q = p[..., :o0].reshape(B, T, FOX_KV_HEADS, FOX_GROUP, FOX_HEAD_DIM)
    k = p[..., o0:o1].reshape(B, T, FOX_KV_HEADS, FOX_HEAD_DIM)
    v = p[..., o1:o2].reshape(B, T, FOX_KV_HEADS, FOX_HEAD_DIM)
    z = p[..., o2:o3]
    logf = jax.nn.log_sigmoid((p[..., o3:] + b_f).astype(jnp.float32))
    return q, k, v, z, logf


def _fox_prompt(h, w_in, b_f, w_out):
    B, S = h.shape[:2]
    q, k, v, z, logf = _fox_project(h, w_in, b_f)
    cum = jnp.cumsum(logf, axis=1).reshape(B, S, FOX_KV_HEADS, FOX_GROUP)
    o = _block_causal_attention(q, k, v, FOX_SCALE, cum)
    y = _gated_out(o.reshape(B, S, FOX_WIDTH), z, w_out)
    return y, k, v, logf


def _fox_sample(h, k_pool, v_pool, logf_pool, page_table, w_in, b_f, w_out):
    Bd, T = h.shape[:2]
    q, k, v, z, logf = _fox_project(h, w_in, b_f)
    k_past = _gather_pages(k_pool, page_table)
    v_past = _gather_pages(v_pool, page_table)
    lf_past = _gather_pages(logf_pool, page_table).astype(jnp.float32)
    P = k_past.shape[1]
    cum_past = jnp.cumsum(lf_past, axis=1)
    cum_new = cum_past[:, -1:] + jnp.cumsum(logf, axis=1)
    cq = jnp.moveaxis(cum_new.reshape(Bd, T, FOX_KV_HEADS, FOX_GROUP), 1, -1)[..., :, None]
    ck_past = jnp.moveaxis(cum_past.reshape(Bd, P, FOX_KV_HEADS, FOX_GROUP), 1, -1)[..., None, :]
    ck_new = jnp.moveaxis(cum_new.reshape(Bd, T, FOX_KV_HEADS, FOX_GROUP), 1, -1)[..., None, :]
    s_past = jnp.einsum('btkgd,bskd->bkgts', q, k_past).astype(jnp.float32) * FOX_SCALE + (cq - ck_past)
    s_new = jnp.einsum('btkgd,bukd->bkgtu', q, k).astype(jnp.float32) * FOX_SCALE + (cq - ck_new)
    s_new = jnp.where(jnp.tril(jnp.ones((T, T), dtype=bool)), s_new, -jnp.inf)
    p = jax.nn.softmax(jnp.concatenate([s_past, s_new], axis=-1), axis=-1).astype(v.dtype)
    o = (jnp.einsum('bkgts,bskd->btkgd', p[..., :P], v_past)
         + jnp.einsum('bkgtu,bukd->btkgd', p[..., P:], v))
    y = _gated_out(o.reshape(Bd, T, FOX_WIDTH), z, w_out)
    return y, k, v, logf


def _conv_project(h, w_in):
    p = h @ w_in
    a = p[..., :CONV_CHANNELS]
    b = p[..., CONV_CHANNELS:2 * CONV_CHANNELS]
    z = p[..., 2 * CONV_CHANNELS:]
    return a * jax.nn.sigmoid(b), z


def _conv_tail(u_ext, z, dw, dw_b, ln_g, ln_b, w_out):
    c = lax.conv_general_dilated(
        u_ext, dw[:, None, :].astype(u_ext.dtype), window_strides=(1,), padding='VALID',
        dimension_numbers=('NWC', 'WIO', 'NWC'), feature_group_count=u_ext.shape[-1]) + dw_b
    c = jax.nn.silu(_layer_norm(c, ln_g, ln_b))
    return _gated_out(c, z, w_out)


def _conv_prompt(h, w_in, dw, dw_b, ln_g, ln_b, w_out):
    u, z = _conv_project(h, w_in)
    u_ext = jnp.pad(u, ((0, 0), (CONV_TAPS - 1, 0), (0, 0)))
    y = _conv_tail(u_ext, z, dw, dw_b, ln_g, ln_b, w_out)
    return y, u[:, -(CONV_TAPS - 1):]


def _conv_sample(h, state, w_in, dw, dw_b, ln_g, ln_b, w_out):
    u, z = _conv_project(h, w_in)
    u_ext = jnp.concatenate([state.astype(u.dtype), u], axis=1)
    y = _conv_tail(u_ext, z, dw, dw_b, ln_g, ln_b, w_out)
    return y, u_ext[:, -(CONV_TAPS - 1):]


def _mla_project(h, pos, w_in, q_norm, w_uq, kv_norm):
    B, T = h.shape[:2]
    p = h @ w_in
    o0 = MLA_Q_LORA
    o1 = o0 + MLA_KV_LORA
    o2 = o1 + MLA_ROPE
    cq = _rms_norm(p[..., :o0], q_norm)
    ckv = _rms_norm(p[..., o0:o1], kv_norm)
    kpe = _rope(p[..., o1:o2][:, :, None, :], pos)[:, :, 0, :]
    z = p[..., o2:]
    q = (cq @ w_uq).reshape(B, T, MLA_HEADS, MLA_NOPE + MLA_ROPE)
    q_nope = q[..., :MLA_NOPE]
    q_pe = _rope(q[..., MLA_NOPE:], pos)
    return q_nope, q_pe, ckv, kpe, z


def _mla_prompt(h, w_in, q_norm, w_uq, kv_norm, w_ukv, w_out):
    B, S = h.shape[:2]
    q_nope, q_pe, ckv, kpe, z = _mla_project(h, jnp.arange(S), w_in, q_norm, w_uq, kv_norm)
    kv = (ckv @ w_ukv).reshape(B, S, MLA_HEADS, MLA_NOPE + MLA_V)
    k = jnp.concatenate(
        [kv[..., :MLA_NOPE], jnp.broadcast_to(kpe[:, :, None, :], (B, S, MLA_HEADS, MLA_ROPE))], axis=-1)
    q = jnp.concatenate([q_nope, q_pe], axis=-1)[:, :, :, None, :]
    o = _block_causal_attention(q, k, kv[..., MLA_NOPE:], MLA_SCALE, None)
    y = _gated_out(o.reshape(B, S, MLA_WIDTH), z, w_out)
    return y, ckv, kpe


def _mla_sample(h, ckv_pool, kpe_pool, page_table, w_in, q_norm, w_uq, kv_norm, w_ukv, w_out):
    Bd, T = h.shape[:2]
    ckv_past = _gather_pages(ckv_pool, page_table)
    kpe_past = _gather_pages(kpe_pool, page_table)
    P = ckv_past.shape[1]
    q_nope, q_pe, ckv, kpe, z = _mla_project(h, P + jnp.arange(T), w_in, q_norm, w_uq, kv_norm)
    w_ukv_h = w_ukv.reshape(MLA_KV_LORA, MLA_HEADS, MLA_NOPE + MLA_V)
    w_uk = w_ukv_h[..., :MLA_NOPE]
    w_uv = w_ukv_h[..., MLA_NOPE:]
    q_lat = jnp.einsum('bthn,lhn->bthl', q_nope, w_uk)
    s_past = (jnp.einsum('bthl,bsl->bhts', q_lat, ckv_past)
              + jnp.einsum('bthr,bsr->bhts', q_pe, kpe_past)).astype(jnp.float32) * MLA_SCALE
    s_new = (jnp.einsum('bthl,bul->bhtu', q_lat, ckv)
             + jnp.einsum('bthr,bur->bhtu', q_pe, kpe)).astype(jnp.float32) * MLA_SCALE
    s_new = jnp.where(jnp.tril(jnp.ones((T, T), dtype=bool)), s_new, -jnp.inf)
    p = jax.nn.softmax(jnp.concatenate([s_past, s_new], axis=-1), axis=-1).astype(ckv.dtype)
    o_lat = (jnp.einsum('bhts,bsl->bthl', p[..., :P], ckv_past)
             + jnp.einsum('bhtu,bul->bthl', p[..., P:], ckv))
    o = jnp.einsum('bthl,lhv->bthv', o_lat, w_uv)
    y = _gated_out(o.reshape(Bd, T, MLA_WIDTH), z, w_out)
    return y, ckv, kpe


def setup_inputs(seed: int = 0) -> dict:
    key = jax.random.key(seed)
    keys = jax.random.split(key, 64)
    counter = [0]

    def nxt():
        k = keys[counter[0]]
        counter[0] += 1
        return k

    def normal(shape, scale=1.0):
        return jax.random.normal(nxt(), shape, jnp.float32) * scale

    def weight(fan_in, fan_out):
        return normal((fan_in, fan_out), fan_in ** -0.5)

    def gain(n):
        return 1.0 + normal((n,), 0.02)

    def bias(n):
        return normal((n,), 0.02)

    n_pages = PAST_LEN // PAGE_SIZE
    n_used = DEC_BATCH * n_pages
    n_pool = n_used + n_used // 4
    page_table = jax.random.permutation(nxt(), n_pool)[:n_used].reshape(DEC_BATCH, n_pages).astype(jnp.int32)

    def logf_pool():
        return jax.nn.log_sigmoid(FORGET_BIAS_INIT + normal((n_pool, PAGE_SIZE, FOX_HEADS)))

    return {
        'x_prompt': normal((BATCH, SEQ, D_MODEL)),
        'x_sample': normal((DEC_BATCH, DEC_SEQ, D_MODEL)),
        'cache_k_l0': normal((n_pool, PAGE_SIZE, FOX_KV_HEADS, FOX_HEAD_DIM)),
        'cache_v_l0': normal((n_pool, PAGE_SIZE, FOX_KV_HEADS, FOX_HEAD_DIM)),
        'cache_logf_l0': logf_pool(),
        'state_conv_l1': normal((DEC_BATCH, CONV_TAPS - 1, CONV_CHANNELS), 0.5),
        'cache_ckv_l2': normal((n_pool, PAGE_SIZE, MLA_KV_LORA)),
        'cache_kpe_l2': normal((n_pool, PAGE_SIZE, MLA_ROPE)),
        'cache_k_l3': normal((n_pool, PAGE_SIZE, FOX_KV_HEADS, FOX_HEAD_DIM)),
        'cache_v_l3': normal((n_pool, PAGE_SIZE, FOX_KV_HEADS, FOX_HEAD_DIM)),
        'cache_logf_l3': logf_pool(),
        'page_table': page_table,
        'norm_pre_l0': gain(D_MODEL),
        'w_in_l0': weight(D_MODEL, FOX_IN),
        'b_f_l0': FORGET_BIAS_INIT + normal((FOX_HEADS,), 0.1),
        'w_out_l0': weight(FOX_WIDTH, D_MODEL),
        'norm_post_l0': gain(D_MODEL),
        'norm_pre_l1': gain(D_MODEL),
        'w_in_l1': weight(D_MODEL, 3 * CONV_CHANNELS),
        'dw_l1': normal((CONV_TAPS, CONV_CHANNELS), CONV_TAPS ** -0.5),
        'dw_b_l1': bias(CONV_CHANNELS),
        'ln_g_l1': gain(CONV_CHANNELS),
        'ln_b_l1': bias(CONV_CHANNELS),
        'w_out_l1': weight(CONV_CHANNELS, D_MODEL),
        'norm_post_l1': gain(D_MODEL),
        'norm_pre_l2': gain(D_MODEL),
        'w_in_l2': weight(D_MODEL, MLA_IN),
        'q_norm_l2': gain(MLA_Q_LORA),
        'w_uq_l2': weight(MLA_Q_LORA, MLA_HEADS * (MLA_NOPE + MLA_ROPE)),
        'kv_norm_l2': gain(MLA_KV_LORA),
        'w_ukv_l2': weight(MLA_KV_LORA, MLA_HEADS * (MLA_NOPE + MLA_V)),
        'w_out_l2': weight(MLA_WIDTH, D_MODEL),
        'norm_post_l2': gain(D_MODEL),
        'norm_pre_l3': gain(D_MODEL),
        'w_in_l3': weight(D_MODEL, FOX_IN),
        'b_f_l3': FORGET_BIAS_INIT + normal((FOX_HEADS,), 0.1),
        'w_out_l3': weight(FOX_WIDTH, D_MODEL),
        'norm_post_l3': gain(D_MODEL),
    }


def reference(x_prompt, x_sample, cache_k_l0, cache_v_l0, cache_logf_l0, state_conv_l1,
              cache_ckv_l2, cache_kpe_l2, cache_k_l3, cache_v_l3, cache_logf_l3, page_table,
              norm_pre_l0, w_in_l0, b_f_l0, w_out_l0, norm_post_l0,
              norm_pre_l1, w_in_l1, dw_l1, dw_b_l1, ln_g_l1, ln_b_l1, w_out_l1, norm_post_l1,
              norm_pre_l2, w_in_l2, q_norm_l2, w_uq_l2, kv_norm_l2, w_ukv_l2, w_out_l2, norm_post_l2,
              norm_pre_l3, w_in_l3, b_f_l3, w_out_l3, norm_post_l3):
    pre = (norm_pre_l0, norm_pre_l1, norm_pre_l2, norm_pre_l3)
    post = (norm_post_l0, norm_post_l1, norm_post_l2, norm_post_l3)
    params = (
        (w_in_l0, b_f_l0, w_out_l0),
        (w_in_l1, dw_l1, dw_b_l1, ln_g_l1, ln_b_l1, w_out_l1),
        (w_in_l2, q_norm_l2, w_uq_l2, kv_norm_l2, w_ukv_l2, w_out_l2),
        (w_in_l3, b_f_l3, w_out_l3),
    )
    caches = (
        (cache_k_l0, cache_v_l0, cache_logf_l0),
        (state_conv_l1,),
        (cache_ckv_l2, cache_kpe_l2),
        (cache_k_l3, cache_v_l3, cache_logf_l3),
    )
    xp, xs = x_prompt, x_sample
    new_prompt, new_sample = [], []
    for i in range(DEPTH):
        kind = i % N_MIXERS
        hp = _rms_norm(xp, pre[i])
        hs = _rms_norm(xs, pre[i])
        if kind == 0:
            yp, *sp = _fox_prompt(hp, *params[i])
            ys, *ss = _fox_sample(hs, *caches[i], page_table, *params[i])
        elif kind == 1:
            yp, *sp = _conv_prompt(hp, *params[i])
            ys, *ss = _conv_sample(hs, *caches[i], *params[i])
        else:
            yp, *sp = _mla_prompt(hp, *params[i])
            ys, *ss = _mla_sample(hs, *caches[i], page_table, *params[i])
        xp = xp + _rms_norm(yp, post[i])
        xs = xs + _rms_norm(ys, post[i])
        new_prompt.append(sp)
        new_sample.append(ss)
    (k_p0, v_p0, lf_p0), (conv_p1,), (ckv_p2, kpe_p2), (k_p3, v_p3, lf_p3) = new_prompt
    (k_s0, v_s0, lf_s0), (conv_s1,), (ckv_s2, kpe_s2), (k_s3, v_s3, lf_s3) = new_sample
    return (xp, xs,
            k_p0, v_p0, lf_p0, k_s0, v_s0, lf_s0,
            conv_p1, conv_s1,
            ckv_p2, kpe_p2, ckv_s2, kpe_s2,
            k_p3, v_p3, lf_p3, k_s3, v_s3, lf_s3)
```

```python
import functools

import numpy as np
import jax
import jax.numpy as jnp
from jax import lax
from jax.experimental import pallas as pl
from jax.experimental.pallas import tpu as pltpu

F32 = jnp.float32
BF16 = jnp.bfloat16

NORM_EPS = 1e-6
ROPE_THETA = 10000.0
CONV_TAPS = 31

FOX_HEADS = 16
FOX_KV_HEADS = 4
FOX_GROUP = FOX_HEADS // FOX_KV_HEADS
HEAD_DIM = 64
MLA_HEADS = 16
MLA_ROPE = 32
MLA_Q_LORA = 768
MLA_KV_LORA = 256
MLA_QK_PAD = 128

V7X_LANES = 128
V7X_VMEM_BYTES = 64 * 1024 * 1024
VMEM_LIMIT = V7X_VMEM_BYTES * 7 // 8

ROW_TILE = 512
ATTN_BLOCK = 256
CONV_ROW_TILE = 256
CONV_CHUNK = 32
CONV_HALO = 32
CUM_GROUP = 256
DEC_ROWS = 64


def _dot(a, b):
    return jnp.dot(a, b, preferred_element_type=F32)


def _dot_nt(a, b):
    return lax.dot_general(a, b, (((1,), (1,)), ((), ())), preferred_element_type=F32)


def _rms(x, g):
    return x * lax.rsqrt(jnp.mean(x * x, axis=-1, keepdims=True) + NORM_EPS) * g


def _log_sigmoid(x):
    return jnp.minimum(x, 0.0) - jnp.log1p(jnp.exp(-jnp.abs(x)))


def _silu(x):
    return x * jax.nn.sigmoid(x)


def _split3(x):
    x1 = x.astype(BF16)
    r = x - x1.astype(F32)
    x2 = r.astype(BF16)
    r = r - x2.astype(F32)
    return x1, x2, r.astype(BF16)


def _upper_tri(n):
    r = lax.broadcasted_iota(jnp.int32, (n, n), 0)
    c = lax.broadcasted_iota(jnp.int32, (n, n), 1)
    return (r <= c).astype(BF16)


def _cumsum_lanes(xt, u, carry):
    g = u.shape[0]
    parts = _split3(xt)
    cums = []
    for i in range(xt.shape[1] // g):
        loc = sum(_dot(p[:, i * g:(i + 1) * g], u) for p in parts)
        cg = loc + carry
        carry = cg[:, g - 1:g]
        cums.append(cg)
    return (cums[0] if len(cums) == 1 else jnp.concatenate(cums, axis=1)), carry


def _params(*sem):
    return pltpu.CompilerParams(dimension_semantics=sem, vmem_limit_bytes=VMEM_LIMIT)


def _full(shape):
    n = len(shape)
    return pl.BlockSpec(shape, lambda *_: (0,) * n)


def _fox_proj_kernel(x_ref, g_ref, wq_ref, wk_ref, wv_ref, wz_ref, wf_ref, bf_ref, *rest,
                     prompt, tiles_per_seq):
    if prompt:
        (wft_ref, bft_ref, u_ref, q_ref, k_ref, v_ref, z_ref, lf_ref,
         kh_ref, vh_ref, nb_ref, carry_ref) = rest
    else:
        q_ref, k_ref, v_ref, z_ref, lf_ref = rest
    h = _rms(x_ref[...], g_ref[...]).astype(BF16)
    q_ref[...] = (_dot(h, wq_ref[...]) * (HEAD_DIM ** -0.5)).astype(BF16)
    k = _dot(h, wk_ref[...])
    v = _dot(h, wv_ref[...])
    k_ref[...] = k
    v_ref[...] = v
    z_ref[...] = _dot(h, wz_ref[...]).astype(BF16)
    f = _dot(h, wf_ref[...])[:, :FOX_HEADS] + bf_ref[...]
    lf_ref[...] = _log_sigmoid(f)
    if prompt:
        for hh in range(FOX_KV_HEADS):
            kh_ref[0, hh] = k[:, hh * HEAD_DIM:(hh + 1) * HEAD_DIM].astype(BF16)
            vh_ref[0, hh] = v[:, hh * HEAD_DIM:(hh + 1) * HEAD_DIM].astype(BF16)

        @pl.when(pl.program_id(0) % tiles_per_seq == 0)
        def _():
            carry_ref[...] = jnp.zeros_like(carry_ref)

        lft = _log_sigmoid(_dot_nt(wft_ref[...], h) + bft_ref[...])
        cum, carry = _cumsum_lanes(lft, u_ref[...], carry_ref[:, 0:1])
        nb_ref[0] = -cum
        carry_ref[...] = jnp.broadcast_to(carry, carry_ref.shape)


def _fox_proj(x, g, w, b_f, *, batch=None):
    rows, d = x.shape
    prompt = batch is not None
    tm = min(ROW_TILE, rows)
    nt = rows // tm
    wq, wk, wv, wz, wf, wft = w
    kvw = wk.shape[1]
    ins = [x, g.reshape(1, d), wq, wk, wv, wz, wf, b_f.reshape(1, FOX_HEADS)]
    in_specs = [pl.BlockSpec((tm, d), lambda i: (i, 0)), _full((1, d)), _full(wq.shape), _full(wk.shape),
                _full(wv.shape), _full(wz.shape), _full(wf.shape), _full((1, FOX_HEADS))]
    out_shape = [jax.ShapeDtypeStruct((rows, wq.shape[1]), BF16), jax.ShapeDtypeStruct((rows, kvw), F32),
                 jax.ShapeDtypeStruct((rows, kvw), F32), jax.ShapeDtypeStruct((rows, wz.shape[1]), BF16),
                 jax.ShapeDtypeStruct((rows, FOX_HEADS), F32)]
    out_specs = [pl.BlockSpec((tm, wq.shape[1]), lambda i: (i, 0)), pl.BlockSpec((tm, kvw), lambda i: (i, 0)),
                 pl.BlockSpec((tm, kvw), lambda i: (i, 0)), pl.BlockSpec((tm, wz.shape[1]), lambda i: (i, 0)),
                 pl.BlockSpec((tm, FOX_HEADS), lambda i: (i, 0))]
    scratch = []
    tps = 1
    if prompt:
        seq = rows // batch
        tps = seq // tm
        ins += [wft, b_f.reshape(FOX_HEADS, 1), _upper_tri_host(tm)]
        in_specs += [_full(wft.shape), _full((FOX_HEADS, 1)), _full((tm, tm))]
        out_shape += [jax.ShapeDtypeStruct((batch, FOX_KV_HEADS, seq, HEAD_DIM), BF16),
                      jax.ShapeDtypeStruct((batch, FOX_KV_HEADS, seq, HEAD_DIM), BF16),
                      jax.ShapeDtypeStruct((batch, FOX_HEADS, seq), F32)]
        out_specs += [pl.BlockSpec((1, FOX_KV_HEADS, tm, HEAD_DIM), lambda i: (i // tps, 0, i % tps, 0)),
                      pl.BlockSpec((1, FOX_KV_HEADS, tm, HEAD_DIM), lambda i: (i // tps, 0, i % tps, 0)),
                      pl.BlockSpec((1, FOX_HEADS, tm), lambda i: (i // tps, 0, i % tps))]
        scratch = [pltpu.VMEM((FOX_HEADS, V7X_LANES), F32)]
    return pl.pallas_call(
        functools.partial(_fox_proj_kernel, prompt=prompt, tiles_per_seq=tps),
        grid=(nt,), in_specs=in_specs, out_specs=out_specs, out_shape=out_shape,
        scratch_shapes=scratch, compiler_params=_params("arbitrary"),
        name="fox_proj_prompt" if prompt else "fox_proj_sample",
    )(*ins)


def _upper_tri_host(n):
    return jnp.asarray(np.triu(np.ones((n, n), np.float32)), dtype=BF16)


def _attn_kernel(*refs, heads, shared_kv, dk, blk, has_bias):
    if has_bias:
        q_ref, k_ref, v_ref, nb_ref, o_ref = refs
    else:
        q_ref, k_ref, v_ref, o_ref = refs
    gi = pl.program_id(1)
    qi = pl.program_id(2)
    row = lax.broadcasted_iota(jnp.int32, (blk, blk), 0)
    col = lax.broadcasted_iota(jnp.int32, (blk, blk), 1)
    for g in range(heads):
        kv = 0 if shared_kv else g
        qg = q_ref[:, g * dk:(g + 1) * dk]

        def scores(j, kv=kv, qg=qg, g=g):
            off = pl.multiple_of(j * blk, blk)
            s = _dot_nt(qg, k_ref[0, kv, pl.ds(off, blk), :])
            if has_bias:
                s = s + nb_ref[0, pl.ds(gi * heads + g, 1), pl.ds(off, blk)]
            return s, v_ref[0, kv, pl.ds(off, blk), :]

        s, vj = scores(qi)
        s = jnp.where(row >= col, s, -jnp.inf)
        m = jnp.max(s, axis=1, keepdims=True)
        p = jnp.exp(s - m)
        l = jnp.sum(p, axis=1, keepdims=True)
        acc = _dot(p.astype(BF16), vj)

        def body(j, carry):
            m, l, acc = carry
            s, vj = scores(j)
            m_new = jnp.maximum(m, jnp.max(s, axis=1, keepdims=True))
            alpha = jnp.exp(m - m_new)
            p = jnp.exp(s - m_new)
            l = alpha * l + jnp.sum(p, axis=1, keepdims=True)
            acc = alpha * acc + _dot(p.astype(BF16), vj)
            return m_new, l, acc

        m, l, acc = lax.fori_loop(0, qi, body, (m, l, acc))
        o_ref[:, g * HEAD_DIM:(g + 1) * HEAD_DIM] = (acc / l).astype(BF16)


def _causal_attention(q, kh, vh, nb, *, heads, shared_kv, dk):
    batch, nkv, seq, _ = kh.shape
    kvb = 1 if shared_kv else heads
    groups = nkv // kvb
    blk = min(ATTN_BLOCK, seq)
    nq = seq // blk
    ins = [q, kh, vh]
    in_specs = [pl.BlockSpec((blk, heads * dk), lambda b, g, i: (b * nq + i, g)),
                pl.BlockSpec((1, kvb, seq, dk), lambda b, g, i: (b, g, 0, 0)),
                pl.BlockSpec((1, kvb, seq, HEAD_DIM), lambda b, g, i: (b, g, 0, 0))]
    if nb is not None:
        ins.append(nb)
        in_specs.append(pl.BlockSpec((1, nb.shape[1], seq), lambda b, g, i: (b, 0, 0)))
    return pl.pallas_call(
        functools.partial(_attn_kernel, heads=heads, shared_kv=shared_kv, dk=dk, blk=blk,
                          has_bias=nb is not None),
        grid=(batch, groups, nq), in_specs=in_specs,
        out_specs=pl.BlockSpec((blk, heads * HEAD_DIM), lambda b, g, i: (b * nq + i, g)),
        out_shape=jax.ShapeDtypeStruct((batch * seq, groups * heads * HEAD_DIM), BF16),
        compiler_params=_params("parallel", "parallel", "arbitrary"),
        name="causal_attention_bias" if nb is not None else "causal_attention",
    )(*ins)


def _out_kernel(o_ref, z_ref, x_ref, w_ref, g_ref, y_ref):
    a = (o_ref[...].astype(F32) * _silu(z_ref[...].astype(F32))).astype(BF16)
    y_ref[...] = x_ref[...] + _rms(_dot(a, w_ref[...]), g_ref[...])


def _gated_out(o, z, x, w, g):
    rows, d = x.shape
    tm = min(ROW_TILE, rows)
    width = o.shape[1]
    return pl.pallas_call(
        _out_kernel, grid=(rows // tm,),
        in_specs=[pl.BlockSpec((tm, width), lambda i: (i, 0)), pl.BlockSpec((tm, width), lambda i: (i, 0)),
                  pl.BlockSpec((tm, d), lambda i: (i, 0)), _full(w.shape), _full((1, d))],
        out_specs=pl.BlockSpec((tm, d), lambda i: (i, 0)),
        out_shape=jax.ShapeDtypeStruct((rows, d), F32),
        compiler_params=_params("parallel"), name="gated_out",
    )(o, z, x, w, g.reshape(1, d))


def _decode_kernel(pt_ref, *refs, n_key_parts, has_bias, pages, select_kv_head):
    del pt_ref
    it = iter(refs)
    q_refs = [next(it) for _ in range(n_key_parts)]
    knew_refs = [next(it) for _ in range(n_key_parts)]
    vnew_ref = next(it) if has_bias else None
    lfnew_ref = next(it) if has_bias else None
    key_pages = [[next(it) for _ in range(pages)] for _ in range(n_key_parts)]
    v_pages = [next(it) for _ in range(pages)] if has_bias else None
    lf_pages = [next(it) for _ in range(pages)] if has_bias else None
    o_ref, m_ref, l_ref, acc_ref = next(it), next(it), next(it), next(it)
    carry_ref = next(it) if has_bias else None

    c = pl.program_id(1)
    page = key_pages[0][0].shape[1]
    n_new = knew_refs[0].shape[1]

    @pl.when(c == 0)
    def _():
        m_ref[...] = jnp.full_like(m_ref, -jnp.inf)
        l_ref[...] = jnp.zeros_like(l_ref)
        acc_ref[...] = jnp.zeros_like(acc_ref)
        if has_bias:
            carry_ref[...] = jnp.zeros_like(carry_ref)

    qs = [r[0] for r in q_refs]
    if has_bias:
        eye = (lax.broadcasted_iota(jnp.int32, (FOX_HEADS, FOX_HEADS), 0)
               == lax.broadcasted_iota(jnp.int32, (FOX_HEADS, FOX_HEADS), 1)).astype(BF16)
        u = _upper_tri(min(CUM_GROUP, pages * page))

    def neg_cum(x, carry):
        xt = sum(_dot_nt(eye, p) for p in _split3(x))
        cum, carry = _cumsum_lanes(xt, u[:min(u.shape[0], x.shape[0]), :min(u.shape[0], x.shape[0])], carry)
        return -cum, carry

    def update(state, s, values):
        m, l, acc = state
        m_new = jnp.maximum(m, jnp.max(s, axis=1, keepdims=True))
        alpha = jnp.exp(m - m_new)
        p = jnp.exp(s - m_new)
        l = alpha * l + jnp.sum(p, axis=1, keepdims=True)
        pb = p.astype(BF16)
        pv = sum(_dot(pb[:, j * page:(j + 1) * page], vj) for j, vj in enumerate(values))
        return m_new, l, alpha * acc + pv

    kb = [[r[0].astype(BF16) for r in part] for part in key_pages]
    s = jnp.concatenate(
        [sum(_dot_nt(qs[i], kb[i][j]) for i in range(n_key_parts)) for j in range(pages)], axis=1)
    carry = None
    if has_bias:
        x = jnp.concatenate([r[0] for r in lf_pages], axis=0)
        nb, carry = neg_cum(x, carry_ref[:, 0:1])
        s = s + jnp.concatenate([nb] * (DEC_ROWS // FOX_HEADS), axis=0)
        carry_ref[...] = jnp.broadcast_to(carry, carry_ref.shape)
        values = [r[0].astype(BF16) for r in v_pages]
    else:
        values = kb[0]
    state = update((m_ref[...], l_ref[...], acc_ref[...]), s, values)
    m_ref[...], l_ref[...], acc_ref[...] = state

    @pl.when(c == pl.num_programs(1) - 1)
    def _():
        def pad(a):
            return jnp.concatenate([a, jnp.zeros((page - n_new, a.shape[1]), a.dtype)], axis=0)

        knew = [pad(r[0]).astype(BF16) for r in knew_refs]
        sn = sum(_dot_nt(qs[i], knew[i]) for i in range(n_key_parts))
        if has_bias:
            nbn, _ = neg_cum(pad(lfnew_ref[0]), carry)
            sn = sn + jnp.concatenate([nbn] * (DEC_ROWS // FOX_HEADS), axis=0)
            vnew = pad(vnew_ref[0]).astype(BF16)
        else:
            vnew = knew[0]
        tok = lax.broadcasted_iota(jnp.int32, sn.shape, 0) // FOX_HEADS
        new = lax.broadcasted_iota(jnp.int32, sn.shape, 1)
        sn = jnp.where(new <= tok, sn, -jnp.inf)
        _, l, acc = update(state, sn, [vnew])
        o = acc / l
        if select_kv_head:
            kvh = (lax.broadcasted_iota(jnp.int32, (DEC_ROWS, HEAD_DIM), 0) % FOX_HEADS) // FOX_GROUP
            o = sum(jnp.where(kvh == k, o[:, k * HEAD_DIM:(k + 1) * HEAD_DIM], 0.0)
                    for k in range(FOX_KV_HEADS))
        o_ref[0] = o


def _page_index(b, c, pt, *, j, pages):
    return (pt[b, c * pages + j], 0, 0)


def _decode_attention(page_table, qs, news, pools, *, vnew=None, lfnew=None, v_pool=None, lf_pool=None):
    bd, n_pages = page_table.shape
    has_bias = lf_pool is not None
    pages = 16 if n_pages % 16 == 0 else 4
    n_parts = len(qs)
    page = pools[0].shape[1]
    dv = v_pool.shape[2] if has_bias else pools[0].shape[2]

    def seq_spec(a):
        return pl.BlockSpec((1,) + a.shape[1:], lambda b, c, pt: (b, 0, 0))

    def page_specs(pool):
        return [pl.BlockSpec((1, page, pool.shape[2]), functools.partial(_page_index, j=j, pages=pages))
                for j in range(pages)]

    ins = list(qs) + list(news)
    in_specs = [seq_spec(a) for a in ins]
    if has_bias:
        ins += [vnew, lfnew]
        in_specs += [seq_spec(vnew), seq_spec(lfnew)]
    for pool in pools:
        ins += [pool] * pages
        in_specs += page_specs(pool)
    scratch = [pltpu.VMEM((DEC_ROWS, 1), F32), pltpu.VMEM((DEC_ROWS, 1), F32), pltpu.VMEM((DEC_ROWS, dv), F32)]
    if has_bias:
        ins += [v_pool] * pages + [lf_pool] * pages
        in_specs += page_specs(v_pool) + page_specs(lf_pool)
        scratch.append(pltpu.VMEM((FOX_HEADS, V7X_LANES), F32))
    out_w = HEAD_DIM if has_bias else dv
    return pl.pallas_call(
        functools.partial(_decode_kernel, n_key_parts=n_parts, has_bias=has_bias, pages=pages,
                          select_kv_head=has_bias),
        grid_spec=pltpu.PrefetchScalarGridSpec(
            num_scalar_prefetch=1, grid=(bd, n_pages // pages), in_specs=in_specs,
            out_specs=pl.BlockSpec((1, DEC_ROWS, out_w), lambda b, c, pt: (b, 0, 0)),
            scratch_shapes=scratch),
        out_shape=jax.ShapeDtypeStruct((bd, DEC_ROWS, out_w), F32),
        compiler_params=_params("parallel", "arbitrary"),
        name="decode_fox" if has_bias else "decode_mla",
    )(page_table, *ins)


def _conv_proj_kernel(x_ref, g_ref, wa_ref, wb_ref, wz_ref, u_ref, z_ref):
    h = _rms(x_ref[...], g_ref[...]).astype(BF16)
    u_ref[...] = _dot(h, wa_ref[...]) * jax.nn.sigmoid(_dot(h, wb_ref[...]))
    z_ref[...] = _dot(h, wz_ref[...]).astype(BF16)


def _conv_proj(x, g, w):
    rows, d = x.shape
    tm = min(ROW_TILE, rows)
    wa, wb, wz = w
    ch = wa.shape[1]
    return pl.pallas_call(
        _conv_proj_kernel, grid=(rows // tm,),
        in_specs=[pl.BlockSpec((tm, d), lambda i: (i, 0)), _full((1, d)), _full(wa.shape), _full(wb.shape),
                  _full(wz.shape)],
        out_specs=[pl.BlockSpec((tm, ch), lambda i: (i, 0)), pl.BlockSpec((tm, ch), lambda i: (i, 0))],
        out_shape=[jax.ShapeDtypeStruct((rows, ch), F32), jax.ShapeDtypeStruct((rows, ch), BF16)],
        compiler_params=_params("parallel"), name="conv_proj",
    )(x, g.reshape(1, d), wa, wb, wz)


def _ln_silu(c, g, b):
    cc = c - jnp.mean(c, axis=-1, keepdims=True)
    var = jnp.mean(cc * cc, axis=-1, keepdims=True)
    return _silu(cc * lax.rsqrt(var + NORM_EPS) * g + b)


def _conv_prompt_kernel(u_ref, halo_ref, dw_ref, dwb_ref, lng_ref, lnb_ref, c_ref, ext_ref, *, tm):
    first = pl.program_id(1) == 0
    ext_ref[0:CONV_HALO, :] = jnp.where(first, 0.0, halo_ref[0])
    ext_ref[CONV_HALO:, :] = u_ref[0]
    base = CONV_HALO - (CONV_TAPS - 1)
    for r0 in range(0, tm, CONV_CHUNK):
        acc = jnp.zeros((CONV_CHUNK, u_ref.shape[2]), F32)
        for j in range(CONV_TAPS):
            acc = acc + dw_ref[j:j + 1, :] * ext_ref[r0 + base + j:r0 + base + j + CONV_CHUNK, :]
        c = _ln_silu(acc + dwb_ref[...], lng_ref[...], lnb_ref[...])
        c_ref[r0:r0 + CONV_CHUNK, :] = c.astype(BF16)


def _conv_prompt(u, dw, dw_b, ln_g, ln_b):
    batch, seq, ch = u.shape
    tm = min(CONV_ROW_TILE, seq)
    nt = seq // tm
    hpt = tm // CONV_HALO
    return pl.pallas_call(
        functools.partial(_conv_prompt_kernel, tm=tm), grid=(batch, nt),
        in_specs=[pl.BlockSpec((1, tm, ch), lambda b, i: (b, i, 0)),
                  pl.BlockSpec((1, CONV_HALO, ch), lambda b, i: (b, jnp.maximum(i * hpt - 1, 0), 0)),
                  _full(dw.shape), _full((1, ch)), _full((1, ch)), _full((1, ch))],
        out_specs=pl.BlockSpec((tm, ch), lambda b, i: (b * nt + i, 0)),
        out_shape=jax.ShapeDtypeStruct((batch * seq, ch), BF16),
        scratch_shapes=[pltpu.VMEM((tm + CONV_HALO, ch), F32)],
        compiler_params=_params("parallel", "arbitrary"), name="conv_prompt",
    )(u, u, dw, dw_b.reshape(1, ch), ln_g.reshape(1, ch), ln_b.reshape(1, ch))


def _conv_sample_kernel(st_ref, u_ref, dw_ref, dwb_ref, lng_ref, lnb_ref, c_ref, so_ref, ext_ref):
    nseq, n_state, _ = st_ref.shape
    t_new = u_ref.shape[1]
    for s in range(nseq):
        ext_ref[s, 0:n_state, :] = st_ref[s]
        ext_ref[s, n_state:n_state + t_new, :] = u_ref[s]
        so_ref[s] = ext_ref[s, t_new:n_state + t_new, :]
        acc = jnp.zeros((t_new, u_ref.shape[2]), F32)
        for j in range(CONV_TAPS):
            acc = acc + dw_ref[j:j + 1, :] * ext_ref[s, j:j + t_new, :]
        c_ref[s] = _ln_silu(acc + dwb_ref[...], lng_ref[...], lnb_ref[...])


def _conv_sample(state, u_new, dw, dw_b, ln_g, ln_b):
    bd, n_state, ch = state.shape
    t_new = u_new.shape[1]
    sb = 8
    return pl.pallas_call(
        _conv_sample_kernel, grid=(bd // sb,),
        in_specs=[pl.BlockSpec((sb, n_state, ch), lambda i: (i, 0, 0)),
                  pl.BlockSpec((sb, t_new, ch), lambda i: (i, 0, 0)),
                  _full(dw.shape), _full((1, ch)), _full((1, ch)), _full((1, ch))],
        out_specs=[pl.BlockSpec((sb, t_new, ch), lambda i: (i, 0, 0)),
                   pl.BlockSpec((sb, n_state, ch), lambda i: (i, 0, 0))],
        out_shape=[jax.ShapeDtypeStruct((bd, t_new, ch), F32), jax.ShapeDtypeStruct((bd, n_state, ch), F32)],
        scratch_shapes=[pltpu.VMEM((sb, n_state + t_new, ch), F32)],
        compiler_params=_params("parallel"), name="conv_sample",
    )(state, u_new, dw, dw_b.reshape(1, ch), ln_g.reshape(1, ch), ln_b.reshape(1, ch))


def _rope_lane_group(x, c, sa, sb):
    return x * c + pltpu.roll(x, V7X_LANES - MLA_ROPE // 2, 1) * sa + pltpu.roll(x, MLA_ROPE // 2, 1) * sb


def _mla_proj_kernel(x_ref, g_ref, wcq_ref, wckv_ref, wkpe_ref, wz_ref, qn_ref, kvn_ref, wuq_ref,
                     c_ref, sa_ref, sb_ref, *rest, prompt):
    if prompt:
        wukn_ref, wuv_ref, q_ref, ckv_ref, kpe_ref, z_ref, kh_ref, vh_ref = rest
    else:
        q_ref, ckv_ref, kpe_ref, z_ref = rest
    scale = (HEAD_DIM + MLA_ROPE) ** -0.5
    c, sa, sb = c_ref[...], sa_ref[...], sb_ref[...]
    h = _rms(x_ref[...], g_ref[...]).astype(BF16)
    cq = _rms(_dot(h, wcq_ref[...]), qn_ref[...]).astype(BF16)
    ckv = _rms(_dot(h, wckv_ref[...]), kvn_ref[...])
    ckv_ref[...] = ckv
    kp = _rope_lane_group(_dot(h, wkpe_ref[...]), c, sa, sb)
    kpe_ref[...] = kp[:, HEAD_DIM:HEAD_DIM + MLA_ROPE]
    z_ref[...] = _dot(h, wz_ref[...]).astype(BF16)
    q = _dot(cq, wuq_ref[...])
    for hd in range(MLA_HEADS):
        sl = slice(hd * MLA_QK_PAD, (hd + 1) * MLA_QK_PAD)
        q_ref[:, sl] = (_rope_lane_group(q[:, sl], c, sa, sb) * scale).astype(BF16)
    if prompt:
        ckv_b = ckv.astype(BF16)
        kn = _dot(ckv_b, wukn_ref[...])
        v = _dot(ckv_b, wuv_ref[...])
        for hd in range(MLA_HEADS):
            kh_ref[0, hd] = (kn[:, hd * MLA_QK_PAD:(hd + 1) * MLA_QK_PAD] + kp).astype(BF16)
            vh_ref[0, hd] = v[:, hd * HEAD_DIM:(hd + 1) * HEAD_DIM].astype(BF16)


def _mla_proj(x, g, w, q_norm, kv_norm, tables, *, batch=None):
    rows, d = x.shape
    prompt = batch is not None
    tm = min(ROW_TILE // 2, rows)
    nt = rows // tm
    wcq, wckv, wkpe, wz, wuq, wukn, wuv = w
    tab_rows = tables[0].shape[0]
    tpt = tab_rows // tm
    ins = [x, g.reshape(1, d), wcq, wckv, wkpe, wz, q_norm.reshape(1, -1), kv_norm.reshape(1, -1), wuq,
           *tables]
    in_specs = [pl.BlockSpec((tm, d), lambda i: (i, 0)), _full((1, d)), _full(wcq.shape), _full(wckv.shape),
                _full(wkpe.shape), _full(wz.shape), _full((1, wcq.shape[1])), _full((1, wckv.shape[1])),
                _full(wuq.shape)]
    in_specs += [pl.BlockSpec((tm, V7X_LANES), lambda i: (i % tpt, 0)) for _ in tables]
    qw = wuq.shape[1]
    out_shape = [jax.ShapeDtypeStruct((rows, qw), BF16), jax.ShapeDtypeStruct((rows, MLA_KV_LORA), F32),
                 jax.ShapeDtypeStruct((rows, MLA_ROPE), F32), jax.ShapeDtypeStruct((rows, wz.shape[1]), BF16)]
    out_specs = [pl.BlockSpec((tm, qw), lambda i: (i, 0)), pl.BlockSpec((tm, MLA_KV_LORA), lambda i: (i, 0)),
                 pl.BlockSpec((tm, MLA_ROPE), lambda i: (i, 0)), pl.BlockSpec((tm, wz.shape[1]), lambda i: (i, 0))]
    if prompt:
        seq = rows // batch
        tps = seq // tm
        ins += [wukn, wuv]
        in_specs += [_full(wukn.shape), _full(wuv.shape)]
        out_shape += [jax.ShapeDtypeStruct((batch, MLA_HEADS, seq, MLA_QK_PAD), BF16),
                      jax.ShapeDtypeStruct((batch, MLA_HEADS, seq, HEAD_DIM), BF16)]
        out_specs += [pl.BlockSpec((1, MLA_HEADS, tm, MLA_QK_PAD), lambda i: (i // tps, 0, i % tps, 0)),
                      pl.BlockSpec((1, MLA_HEADS, tm, HEAD_DIM), lambda i: (i // tps, 0, i % tps, 0))]
    return pl.pallas_call(
        functools.partial(_mla_proj_kernel, prompt=prompt), grid=(nt,), in_specs=in_specs,
        out_specs=out_specs, out_shape=out_shape, compiler_params=_params("parallel"),
        name="mla_proj_prompt" if prompt else "mla_proj_sample",
    )(*ins)


def _rope_tables(positions):
    half = MLA_ROPE // 2
    inv = ROPE_THETA ** (-np.arange(half, dtype=np.float32) / half)
    ang = np.asarray(positions, np.float32)[:, None] * inv[None, :]
    cos, sin = np.cos(ang).astype(np.float32), np.sin(ang).astype(np.float32)
    n = ang.shape[0]
    c = np.zeros((n, V7X_LANES), np.float32)
    sa = np.zeros((n, V7X_LANES), np.float32)
    sb = np.zeros((n, V7X_LANES), np.float32)
    c[:, :HEAD_DIM] = 1.0
    c[:, HEAD_DIM:HEAD_DIM + half] = cos
    c[:, HEAD_DIM + half:HEAD_DIM + 2 * half] = cos
    sa[:, HEAD_DIM:HEAD_DIM + half] = -sin
    sb[:, HEAD_DIM + half:HEAD_DIM + 2 * half] = sin
    return jnp.asarray(c), jnp.asarray(sa), jnp.asarray(sb)


def _headwise_kernel(x_ref, w_ref, o_ref):
    o_ref[0] = _dot(x_ref[0], w_ref[0]).astype(o_ref.dtype)


def _headwise_matmul(x, w, dtype):
    nh, rows, k = x.shape
    n = w.shape[2]
    return pl.pallas_call(
        _headwise_kernel, grid=(nh,),
        in_specs=[pl.BlockSpec((1, rows, k), lambda i: (i, 0, 0)), pl.BlockSpec((1, k, n), lambda i: (i, 0, 0))],
        out_specs=pl.BlockSpec((1, rows, n), lambda i: (i, 0, 0)),
        out_shape=jax.ShapeDtypeStruct((nh, rows, n), dtype),
        compiler_params=_params("parallel"), name="headwise_matmul",
    )(x, w)


def _fox_weights(w_in):
    d = w_in.shape[0]
    width = FOX_HEADS * HEAD_DIM
    kvw = FOX_KV_HEADS * HEAD_DIM
    o1, o2, o3 = width + kvw, width + 2 * kvw, 2 * width + 2 * kvw
    wb = w_in.astype(BF16)
    wf = jnp.zeros((d, V7X_LANES), BF16).at[:, :FOX_HEADS].set(wb[:, o3:])
    return wb[:, :width], wb[:, width:o1], wb[:, o1:o2], wb[:, o2:o3], wf, wb[:, o3:].T


def _fox_layer(xp, xs, cache_k, cache_v, cache_logf, page_table, g_pre, w_in, b_f, w_out, g_post, dims):
    batch, seq, bd, t_new = dims
    w = _fox_weights(w_in)
    w_out = w_out.astype(BF16)
    n_pool, page = cache_k.shape[:2]
    kvw = FOX_KV_HEADS * HEAD_DIM

    q, k, v, z, lf, kh, vh, nb = _fox_proj(xp, g_pre, w, b_f, batch=batch)
    o = _causal_attention(q, kh, vh, nb, heads=FOX_GROUP, shared_kv=True, dk=HEAD_DIM)
    xp = _gated_out(o, z, xp, w_out, g_post)

    qs, ks, vs, zs, lfs = _fox_proj(xs, g_pre, w, b_f)
    own = (np.arange(FOX_HEADS)[:, None] // FOX_GROUP == np.arange(FOX_KV_HEADS)[None, :])
    qbd = (qs.reshape(bd, t_new, FOX_HEADS, 1, HEAD_DIM)
           * jnp.asarray(own, BF16)[None, None, :, :, None]).reshape(bd, t_new * FOX_HEADS, kvw)
    os_ = _decode_attention(
        page_table, [qbd], [ks.reshape(bd, t_new, kvw)], [cache_k.reshape(n_pool, page, kvw)],
        vnew=vs.reshape(bd, t_new, kvw), lfnew=lfs.reshape(bd, t_new, FOX_HEADS),
        v_pool=cache_v.reshape(n_pool, page, kvw), lf_pool=cache_logf)
    xs = _gated_out(os_.reshape(bd * t_new, FOX_HEADS * HEAD_DIM), zs, xs, w_out, g_post)

    new_p = (k.reshape(batch, seq, FOX_KV_HEADS, HEAD_DIM), v.reshape(batch, seq, FOX_KV_HEADS, HEAD_DIM),
             lf.reshape(batch, seq, FOX_HEADS))
    new_s = (ks.reshape(bd, t_new, FOX_KV_HEADS, HEAD_DIM), vs.reshape(bd, t_new, FOX_KV_HEADS, HEAD_DIM),
             lfs.reshape(bd, t_new, FOX_HEADS))
    return xp, xs, new_p, new_s


def _conv_layer(xp, xs, state, g_pre, w_in, dw, dw_b, ln_g, ln_b, w_out, g_post, dims):
    batch, seq, bd, t_new = dims
    ch = dw.shape[1]
    wb = w_in.astype(BF16)
    w = (wb[:, :ch], wb[:, ch:2 * ch], wb[:, 2 * ch:])
    w_out = w_out.astype(BF16)

    u, z = _conv_proj(xp, g_pre, w)
    u3 = u.reshape(batch, seq, ch)
    c = _conv_prompt(u3, dw, dw_b, ln_g, ln_b)
    xp = _gated_out(c, z, xp, w_out, g_post)

    us, zs = _conv_proj(xs, g_pre, w)
    cs, state_new = _conv_sample(state, us.reshape(bd, t_new, ch), dw, dw_b, ln_g, ln_b)
    xs = _gated_out(cs.reshape(bd * t_new, ch), zs, xs, w_out, g_post)
    return xp, xs, (u3[:, seq - (CONV_TAPS - 1):],), (state_new,)


def _mla_layer(xp, xs, cache_ckv, cache_kpe, page_table, g_pre, w_in, q_norm, w_uq, kv_norm, w_ukv, w_out,
               g_post, dims):
    batch, seq, bd, t_new = dims
    d = w_in.shape[0]
    past = page_table.shape[1] * cache_ckv.shape[1]
    o0, o1, o2 = MLA_Q_LORA, MLA_Q_LORA + MLA_KV_LORA, MLA_Q_LORA + MLA_KV_LORA + MLA_ROPE
    wb = w_in.astype(BF16)
    wkpe = jnp.zeros((d, V7X_LANES), BF16).at[:, HEAD_DIM:HEAD_DIM + MLA_ROPE].set(wb[:, o1:o2])
    dqk = HEAD_DIM + MLA_ROPE
    wuq = jnp.pad(w_uq.astype(BF16).reshape(MLA_Q_LORA, MLA_HEADS, dqk),
                  ((0, 0), (0, 0), (0, MLA_QK_PAD - dqk))).reshape(MLA_Q_LORA, MLA_HEADS * MLA_QK_PAD)
    wukv = w_ukv.astype(BF16).reshape(MLA_KV_LORA, MLA_HEADS, 2 * HEAD_DIM)
    wuk, wuv = wukv[..., :HEAD_DIM], wukv[..., HEAD_DIM:]
    wukn = jnp.pad(wuk, ((0, 0), (0, 0), (0, MLA_QK_PAD - HEAD_DIM))).reshape(MLA_KV_LORA, MLA_HEADS * MLA_QK_PAD)
    w = (wb[:, :o0], wb[:, o0:o1], wkpe, wb[:, o2:], wuq, wukn, wuv.reshape(MLA_KV_LORA, MLA_HEADS * HEAD_DIM))
    w_out = w_out.astype(BF16)

    q, ckv, kpe, z, kh, vh = _mla_proj(xp, g_pre, w, q_norm, kv_norm, _rope_tables(np.arange(seq)), batch=batch)
    o = _causal_attention(q, kh, vh, None, heads=2, shared_kv=False, dk=MLA_QK_PAD)
    xp = _gated_out(o, z, xp, w_out, g_post)

    rows = bd * t_new
    pos_s = np.tile(past + np.arange(t_new), bd)
    qs, ckvs, kpes, zs = _mla_proj(xs, g_pre, w, q_norm, kv_norm, _rope_tables(pos_s))
    q4 = qs.reshape(rows, MLA_HEADS, MLA_QK_PAD)
    q_lat = _headwise_matmul(q4[..., :HEAD_DIM].transpose(1, 0, 2), wuk.transpose(1, 2, 0), BF16)
    q_lat = q_lat.transpose(1, 0, 2).reshape(bd, t_new * MLA_HEADS, MLA_KV_LORA)
    q_pe = q4[..., HEAD_DIM:HEAD_DIM + MLA_ROPE].reshape(bd, t_new * MLA_HEADS, MLA_ROPE)
    o_lat = _decode_attention(
        page_table, [q_lat, q_pe],
        [ckvs.reshape(bd, t_new, MLA_KV_LORA), kpes.reshape(bd, t_new, MLA_ROPE)], [cache_ckv, cache_kpe])
    o_lat = o_lat.astype(BF16).reshape(rows, MLA_HEADS, MLA_KV_LORA).transpose(1, 0, 2)
    os_ = _headwise_matmul(o_lat, wuv.transpose(1, 0, 2), BF16).transpose(1, 0, 2)
    xs = _gated_out(os_.reshape(rows, MLA_HEADS * HEAD_DIM), zs, xs, w_out, g_post)

    new_p = (ckv.reshape(batch, seq, MLA_KV_LORA), kpe.reshape(batch, seq, MLA_ROPE))
    new_s = (ckvs.reshape(bd, t_new, MLA_KV_LORA), kpes.reshape(bd, t_new, MLA_ROPE))
    return xp, xs, new_p, new_s


def kernel(x_prompt, x_sample, cache_k_l0, cache_v_l0, cache_logf_l0, state_conv_l1, cache_ckv_l2, cache_kpe_l2, cache_k_l3, cache_v_l3, cache_logf_l3, page_table, norm_pre_l0, w_in_l0, b_f_l0, w_out_l0, norm_post_l0, norm_pre_l1, w_in_l1, dw_l1, dw_b_l1, ln_g_l1, ln_b_l1, w_out_l1, norm_post_l1, norm_pre_l2, w_in_l2, q_norm_l2, w_uq_l2, kv_norm_l2, w_ukv_l2, w_out_l2, norm_post_l2, norm_pre_l3, w_in_l3, b_f_l3, w_out_l3, norm_post_l3):
    batch, seq, d = x_prompt.shape
    bd, t_new, _ = x_sample.shape
    assert t_new * FOX_HEADS == DEC_ROWS and t_new * MLA_HEADS == DEC_ROWS
    dims = (batch, seq, bd, t_new)
    xp = x_prompt.reshape(batch * seq, d)
    xs = x_sample.reshape(bd * t_new, d)

    xp, xs, (k_p0, v_p0, lf_p0), (k_s0, v_s0, lf_s0) = _fox_layer(
        xp, xs, cache_k_l0, cache_v_l0, cache_logf_l0, page_table,
        norm_pre_l0, w_in_l0, b_f_l0, w_out_l0, norm_post_l0, dims)
    xp, xs, (conv_p1,), (conv_s1,) = _conv_layer(
        xp, xs, state_conv_l1, norm_pre_l1, w_in_l1, dw_l1, dw_b_l1, ln_g_l1, ln_b_l1, w_out_l1,
        norm_post_l1, dims)
    xp, xs, (ckv_p2, kpe_p2), (ckv_s2, kpe_s2) = _mla_layer(
        xp, xs, cache_ckv_l2, cache_kpe_l2, page_table,
        norm_pre_l2, w_in_l2, q_norm_l2, w_uq_l2, kv_norm_l2, w_ukv_l2, w_out_l2, norm_post_l2, dims)
    xp, xs, (k_p3, v_p3, lf_p3), (k_s3, v_s3, lf_s3) = _fox_layer(
        xp, xs, cache_k_l3, cache_v_l3, cache_logf_l3, page_table,
        norm_pre_l3, w_in_l3, b_f_l3, w_out_l3, norm_post_l3, dims)

    return (xp.reshape(batch, seq, d), xs.reshape(bd, t_new, d),
            k_p0, v_p0, lf_p0, k_s0, v_s0, lf_s0,
            conv_p1, conv_s1,
            ckv_p2, kpe_p2, ckv_s2, kpe_s2,
            k_p3, v_p3, lf_p3, k_s3, v_s3, lf_s3)
```

```python
import functools

import numpy as np
import jax
import jax.numpy as jnp
from jax import lax
from jax.experimental import pallas as pl
from jax.experimental.pallas import tpu as pltpu

F32 = jnp.float32
BF16 = jnp.bfloat16

NORM_EPS = 1e-6
LOG2E = 1.4426950408889634
ROPE_THETA = 10000.0
CONV_TAPS = 31

FOX_HEADS = 16
FOX_KV_HEADS = 4
FOX_GROUP = FOX_HEADS // FOX_KV_HEADS
HEAD_DIM = 64
MLA_HEADS = 16
MLA_ROPE = 32
MLA_Q_LORA = 768
MLA_KV_LORA = 256
MLA_QK_PAD = 128

V7X_LANES = 128
V7X_SUBLANES = 8
V7X_VMEM_BYTES = 64 * 1024 * 1024
VMEM_LIMIT = V7X_VMEM_BYTES * 7 // 8

ROW_TILE = 512
ATTN_BLOCK = 512
CONV_ROW_TILE = 256
CONV_CHUNK = 32
CONV_HALO = 32
CONV_SEQ_BLOCK = 32
CUM_GROUP = 256
DEC_ROWS = 64
DEC_PAGES = 16


def _dot(a, b):
    return jnp.dot(a, b, preferred_element_type=F32)


def _dot_nt(a, b):
    return lax.dot_general(a, b, (((1,), (1,)), ((), ())), preferred_element_type=F32)


def _rms(x, g):
    return x * lax.rsqrt(jnp.mean(x * x, axis=-1, keepdims=True) + NORM_EPS) * g


def _log_sigmoid(x):
    return jnp.minimum(x, 0.0) - jnp.log1p(jnp.exp(-jnp.abs(x)))


def _silu(x):
    return x * jax.nn.sigmoid(x)


def _split3(x):
    x1 = x.astype(BF16)
    r = x - x1.astype(F32)
    x2 = r.astype(BF16)
    r = r - x2.astype(F32)
    return x1, x2, r.astype(BF16)


def _upper_tri(n):
    r = lax.broadcasted_iota(jnp.int32, (n, n), 0)
    c = lax.broadcasted_iota(jnp.int32, (n, n), 1)
    return (r <= c).astype(BF16)


def _cumsum_lanes(xt, carry, u=None):
    h, n = xt.shape
    g = min(CUM_GROUP, n)
    ng = n // g
    x = xt if ng == 1 else jnp.concatenate([xt[:, i * g:(i + 1) * g] for i in range(ng)], axis=0)
    if u is None:
        u = _upper_tri(g)
    cum = sum(_dot(p, u) for p in _split3(x))
    if ng > 1:
        r = lax.broadcasted_iota(jnp.int32, (ng * h, ng * h), 0)
        c = lax.broadcasted_iota(jnp.int32, (ng * h, ng * h), 1)
        earlier = ((r % h == c % h) & (c // h < r // h)).astype(BF16)
        tot = jnp.broadcast_to(cum[:, g - 1:g], (ng * h, V7X_LANES))
        off = sum(_dot(earlier, p) for p in _split3(tot))
        cum = cum + jnp.concatenate([off] * (g // V7X_LANES), axis=1)
        carry_rows = jnp.concatenate([carry] * ng, axis=0)
    else:
        carry_rows = carry
    cum = cum + carry_rows
    new_carry = cum[(ng - 1) * h:, g - 1:g]
    if ng > 1:
        cum = jnp.concatenate([cum[i * h:(i + 1) * h] for i in range(ng)], axis=1)
    return cum, new_carry


def _params(*sem):
    return pltpu.CompilerParams(dimension_semantics=sem, vmem_limit_bytes=VMEM_LIMIT)


def _full(shape):
    n = len(shape)
    return pl.BlockSpec(shape, lambda *_: (0,) * n)


def _fox_proj_kernel(x_ref, g_ref, wq_ref, wk_ref, wv_ref, wz_ref, *rest, prompt, tiles_per_seq):
    if prompt:
        (wft_ref, bft_ref, u_ref, q_ref, k_ref, v_ref, z_ref, lft_ref,
         kh_ref, vh_ref, nb_ref, carry_ref) = rest
    else:
        wf_ref, bf_ref, q_ref, k_ref, v_ref, z_ref, lf_ref = rest
    h = _rms(x_ref[...], g_ref[...]).astype(BF16)
    q_ref[...] = (_dot(h, wq_ref[...]) * (LOG2E * HEAD_DIM ** -0.5)).astype(BF16)
    k = _dot(h, wk_ref[...])
    v = _dot(h, wv_ref[...])
    k_ref[...] = k
    v_ref[...] = v
    z_ref[...] = _dot(h, wz_ref[...]).astype(BF16)
    if prompt:
        for hh in range(FOX_KV_HEADS):
            kh_ref[0, hh] = k[:, hh * HEAD_DIM:(hh + 1) * HEAD_DIM].astype(BF16)
            vh_ref[0, hh] = v[:, hh * HEAD_DIM:(hh + 1) * HEAD_DIM].astype(BF16)

        @pl.when(pl.program_id(0) % tiles_per_seq == 0)
        def _():
            carry_ref[...] = jnp.zeros_like(carry_ref)

        lft = _log_sigmoid(_dot_nt(wft_ref[...], h) + bft_ref[...])
        lft_ref[0] = lft
        cum, carry = _cumsum_lanes(lft, carry_ref[:, 0:1], u_ref[...])
        nb_ref[0] = -LOG2E * cum
        carry_ref[...] = jnp.broadcast_to(carry, carry_ref.shape)
    else:
        f = _dot(h, wf_ref[...])[:, :FOX_HEADS] + bf_ref[...]
        lf_ref[...] = _log_sigmoid(f)


def _fox_proj(x, g, w, b_f, *, batch=None):
    rows, d = x.shape
    prompt = batch is not None
    tm = min(ROW_TILE, rows)
    nt = rows // tm
    wq, wk, wv, wz, wf, wft = w
    kvw = wk.shape[1]
    ins = [x, g.reshape(1, d), wq, wk, wv, wz]
    in_specs = [pl.BlockSpec((tm, d), lambda i: (i, 0)), _full((1, d)), _full(wq.shape), _full(wk.shape),
                _full(wv.shape), _full(wz.shape)]
    out_shape = [jax.ShapeDtypeStruct((rows, wq.shape[1]), BF16), jax.ShapeDtypeStruct((rows, kvw), F32),
                 jax.ShapeDtypeStruct((rows, kvw), F32), jax.ShapeDtypeStruct((rows, wz.shape[1]), BF16)]
    out_specs = [pl.BlockSpec((tm, wq.shape[1]), lambda i: (i, 0)), pl.BlockSpec((tm, kvw), lambda i: (i, 0)),
                 pl.BlockSpec((tm, kvw), lambda i: (i, 0)), pl.BlockSpec((tm, wz.shape[1]), lambda i: (i, 0))]
    scratch = []
    tps = 1
    if prompt:
        seq = rows // batch
        tps = seq // tm
        ins += [wft, b_f.reshape(FOX_HEADS, 1), _upper_tri_host(min(CUM_GROUP, tm))]
        in_specs += [_full(wft.shape), _full((FOX_HEADS, 1)), _full((min(CUM_GROUP, tm),) * 2)]
        head_major = jax.ShapeDtypeStruct((batch, FOX_KV_HEADS, seq, HEAD_DIM), BF16)
        per_head = jax.ShapeDtypeStruct((batch, FOX_HEADS, seq), F32)
        out_shape += [per_head, head_major, head_major, per_head]
        hm_spec = pl.BlockSpec((1, FOX_KV_HEADS, tm, HEAD_DIM), lambda i: (i // tps, 0, i % tps, 0))
        ph_spec = pl.BlockSpec((1, FOX_HEADS, tm), lambda i: (i // tps, 0, i % tps))
        out_specs += [ph_spec, hm_spec, hm_spec, ph_spec]
        scratch = [pltpu.VMEM((FOX_HEADS, V7X_LANES), F32)]
    else:
        ins += [wf, b_f.reshape(1, FOX_HEADS)]
        in_specs += [_full(wf.shape), _full((1, FOX_HEADS))]
        out_shape.append(jax.ShapeDtypeStruct((rows, FOX_HEADS), F32))
        out_specs.append(pl.BlockSpec((tm, FOX_HEADS), lambda i: (i, 0)))
    return pl.pallas_call(
        functools.partial(_fox_proj_kernel, prompt=prompt, tiles_per_seq=tps),
        grid=(nt,), in_specs=in_specs, out_specs=out_specs, out_shape=out_shape,
        scratch_shapes=scratch, compiler_params=_params("arbitrary"),
        name="fox_proj_prompt" if prompt else "fox_proj_sample",
    )(*ins)


def _upper_tri_host(n):
    return jnp.asarray(np.triu(np.ones((n, n), np.float32)), dtype=BF16)


def _attn_kernel(*refs, heads, shared_kv, dk, blk, has_bias):
    if has_bias:
        q_ref, k_ref, v_ref, nb_ref, o_ref = refs
    else:
        q_ref, k_ref, v_ref, o_ref = refs
    gi = pl.program_id(1)
    qi = pl.program_id(2)
    if shared_kv:
        chains = [(jnp.concatenate([q_ref[:, g * dk:(g + 1) * dk] for g in range(heads)], axis=0), 0, heads)]
    else:
        chains = [(q_ref[:, g * dk:(g + 1) * dk], g, 1) for g in range(heads)]

    def scores(chain, j, masked):
        qc, kv, nh = chain
        off = pl.multiple_of(j * blk, blk)
        s = _dot_nt(qc, k_ref[0, kv, pl.ds(off, blk), :])
        if has_bias:
            tile = nb_ref[0, pl.ds(pl.multiple_of(gi * heads // V7X_SUBLANES * V7X_SUBLANES, V7X_SUBLANES),
                                   V7X_SUBLANES), pl.ds(off, blk)]
            nb = jnp.where(gi % (V7X_SUBLANES // heads) == 0, tile[:heads], tile[heads:])
            s = (s.reshape(nh, blk, blk) + nb[:, None, :]).reshape(nh * blk, blk)
        if masked:
            row = lax.broadcasted_iota(jnp.int32, s.shape, 0) % blk
            col = lax.broadcasted_iota(jnp.int32, s.shape, 1)
            s = jnp.where(row >= col, s, -jnp.inf)
        return s, v_ref[0, kv, pl.ds(off, blk), :]

    def first(chain):
        s, vj = scores(chain, qi, True)
        m = jnp.max(s, axis=1, keepdims=True)
        p = jnp.exp2(s - m)
        return m, jnp.sum(p, axis=1, keepdims=True), _dot(p.astype(BF16), vj)

    def step(chain, j, state):
        m, l, acc = state
        s, vj = scores(chain, j, False)
        m_new = jnp.maximum(m, jnp.max(s, axis=1, keepdims=True))
        alpha = jnp.exp2(m - m_new)
        p = jnp.exp2(s - m_new)
        return (m_new, alpha * l + jnp.sum(p, axis=1, keepdims=True),
                alpha * acc + _dot(p.astype(BF16), vj))

    states = lax.fori_loop(
        0, qi, lambda j, states: tuple(step(c, j, st) for c, st in zip(chains, states)),
        tuple(first(c) for c in chains))
    g = 0
    for (_, _, nh), (_, l, acc) in zip(chains, states):
        o = (acc / l).astype(BF16)
        for i in range(nh):
            o_ref[:, g * HEAD_DIM:(g + 1) * HEAD_DIM] = o[i * blk:(i + 1) * blk]
            g += 1


def _causal_attention(q, kh, vh, nb, *, heads, shared_kv, dk):
    batch, nkv, seq, _ = kh.shape
    kvb = 1 if shared_kv else heads
    groups = nkv // kvb
    blk = min(ATTN_BLOCK, seq)
    nq = seq // blk
    ins = [q, kh, vh]
    in_specs = [pl.BlockSpec((blk, heads * dk), lambda b, g, i: (b * nq + i, g)),
                pl.BlockSpec((1, kvb, seq, dk), lambda b, g, i: (b, g, 0, 0)),
                pl.BlockSpec((1, kvb, seq, HEAD_DIM), lambda b, g, i: (b, g, 0, 0))]
    if nb is not None:
        assert shared_kv and 2 * heads == V7X_SUBLANES
        ins.append(nb)
        in_specs.append(pl.BlockSpec((1, nb.shape[1], seq), lambda b, g, i: (b, 0, 0)))
    return pl.pallas_call(
        functools.partial(_attn_kernel, heads=heads, shared_kv=shared_kv, dk=dk, blk=blk,
                          has_bias=nb is not None),
        grid=(batch, groups, nq), in_specs=in_specs,
        out_specs=pl.BlockSpec((blk, heads * HEAD_DIM), lambda b, g, i: (b * nq + i, g)),
        out_shape=jax.ShapeDtypeStruct((batch * seq, groups * heads * HEAD_DIM), BF16),
        compiler_params=_params("parallel", "parallel", "arbitrary"),
        name="causal_attention_bias" if nb is not None else "causal_attention",
    )(*ins)


def _out_kernel(o_ref, z_ref, x_ref, w_ref, g_ref, y_ref):
    a = (o_ref[...].astype(F32) * _silu(z_ref[...].astype(F32))).astype(BF16)
    y_ref[...] = x_ref[...] + _rms(_dot(a, w_ref[...]), g_ref[...])


def _gated_out(o, z, x, w, g):
    rows, d = x.shape
    tm = min(ROW_TILE, rows)
    width = o.shape[1]
    return pl.pallas_call(
        _out_kernel, grid=(rows // tm,),
        in_specs=[pl.BlockSpec((tm, width), lambda i: (i, 0)), pl.BlockSpec((tm, width), lambda i: (i, 0)),
                  pl.BlockSpec((tm, d), lambda i: (i, 0)), _full(w.shape), _full((1, d))],
        out_specs=pl.BlockSpec((tm, d), lambda i: (i, 0)),
        out_shape=jax.ShapeDtypeStruct((rows, d), F32),
        compiler_params=_params("parallel"), name="gated_out",
    )(o, z, x, w, g.reshape(1, d))


def _decode_kernel(pt_ref, *refs, key_pos_minor, new_pos_minor, has_bias, pages, page):
    del pt_ref
    it = iter(refs)
    n_parts = len(key_pos_minor)
    q_refs = [next(it) for _ in range(n_parts)]
    knew_refs = [next(it) for _ in range(n_parts)]
    vnew_ref = next(it) if has_bias else None
    lfnew_ref = next(it) if has_bias else None
    key_pages = [[next(it) for _ in range(pages)] for _ in range(n_parts)]
    v_pages = [next(it) for _ in range(pages)] if has_bias else None
    lf_pages = [next(it) for _ in range(pages)] if has_bias else None
    o_ref, m_ref, l_ref, acc_ref = next(it), next(it), next(it), next(it)
    carry_ref = next(it) if has_bias else None
    c = pl.program_id(1)

    @pl.when(c == 0)
    def _():
        m_ref[...] = jnp.full_like(m_ref, -jnp.inf)
        l_ref[...] = jnp.zeros_like(l_ref)
        acc_ref[...] = jnp.zeros_like(acc_ref)
        if has_bias:
            carry_ref[...] = jnp.zeros_like(carry_ref)

    qs = [r[0] for r in q_refs]

    def scores(keys, pos_minor):
        return sum((_dot if pm else _dot_nt)(q, k) for q, k, pm in zip(qs, keys, pos_minor))

    def update(state, s, values, values_pos_minor):
        m, l, acc = state
        m_new = jnp.maximum(m, jnp.max(s, axis=1, keepdims=True))
        alpha = jnp.exp2(m - m_new)
        p = jnp.exp2(s - m_new)
        l = alpha * l + jnp.sum(p, axis=1, keepdims=True)
        pb = p.astype(BF16)
        pv = sum((_dot_nt if values_pos_minor else _dot)(pb[:, j * page:(j + 1) * page], vj)
                 for j, vj in enumerate(values))
        return m_new, l, alpha * acc + pv

    def tile_rows(nb):
        return jnp.concatenate([nb] * (DEC_ROWS // FOX_HEADS), axis=0)

    kb = [[r[0].astype(BF16) for r in part] for part in key_pages]
    s = jnp.concatenate([scores([kb[i][j] for i in range(n_parts)], key_pos_minor) for j in range(pages)],
                        axis=1)
    carry = None
    if has_bias:
        cum, carry = _cumsum_lanes(jnp.concatenate([r[0] for r in lf_pages], axis=1), carry_ref[:, 0:1])
        s = s - tile_rows(LOG2E * cum)
        carry_ref[...] = jnp.broadcast_to(carry, carry_ref.shape)
        values = [r[0].astype(BF16) for r in v_pages]
    else:
        values = kb[0]
    state = update((m_ref[...], l_ref[...], acc_ref[...]), s, values, has_bias)
    m_ref[...], l_ref[...], acc_ref[...] = state

    @pl.when(c == pl.num_programs(1) - 1)
    def _():
        def pad(a):
            return jnp.concatenate([a, jnp.zeros((page - a.shape[0], a.shape[1]), a.dtype)], axis=0)

        knew = [(r[0] if pm else pad(r[0])).astype(BF16) for r, pm in zip(knew_refs, new_pos_minor)]
        sn = scores(knew, new_pos_minor)
        if has_bias:
            cum_new, _ = _cumsum_lanes(lfnew_ref[0], carry)
            sn = sn - tile_rows(LOG2E * cum_new)
            vnew = pad(vnew_ref[0]).astype(BF16)
        else:
            vnew = knew[0]
        tok = lax.broadcasted_iota(jnp.int32, sn.shape, 0) // FOX_HEADS
        new = lax.broadcasted_iota(jnp.int32, sn.shape, 1)
        sn = jnp.where(new <= tok, sn, -jnp.inf)
        _, l, acc = update(state, sn, [vnew], False)
        o = acc / l
        if has_bias:
            kvh = (lax.broadcasted_iota(jnp.int32, (DEC_ROWS, HEAD_DIM), 0) % FOX_HEADS) // FOX_GROUP
            o = sum(jnp.where(kvh == k, o[:, k * HEAD_DIM:(k + 1) * HEAD_DIM], 0.0)
                    for k in range(FOX_KV_HEADS))
        o_ref[0] = o


def _page_index(b, c, pt, *, j, pages):
    return (pt[b, c * pages + j], 0, 0)


def _decode_attention(page_table, qs, news, pools, key_pos_minor, new_pos_minor, *,
                      vnew=None, lfnew=None, v_pool=None, lf_pool=None):
    bd, n_pages = page_table.shape
    has_bias = lf_pool is not None
    pages = DEC_PAGES if n_pages % DEC_PAGES == 0 else 4
    page = pools[0].shape[2 if key_pos_minor[0] else 1]
    dv = v_pool.shape[1] if has_bias else pools[0].shape[2]

    def seq_spec(a):
        return pl.BlockSpec((1,) + a.shape[1:], lambda b, c, pt: (b, 0, 0))

    def page_specs(pool):
        return [pl.BlockSpec((1,) + pool.shape[1:], functools.partial(_page_index, j=j, pages=pages))
                for j in range(pages)]

    ins = list(qs) + list(news)
    if has_bias:
        ins += [vnew, lfnew]
    in_specs = [seq_spec(a) for a in ins]
    for pool in pools:
        ins += [pool] * pages
        in_specs += page_specs(pool)
    scratch = [pltpu.VMEM((DEC_ROWS, 1), F32), pltpu.VMEM((DEC_ROWS, 1), F32), pltpu.VMEM((DEC_ROWS, dv), F32)]
    if has_bias:
        ins += [v_pool] * pages + [lf_pool] * pages
        in_specs += page_specs(v_pool) + page_specs(lf_pool)
        scratch.append(pltpu.VMEM((FOX_HEADS, V7X_LANES), F32))
    out_w = HEAD_DIM if has_bias else dv
    return pl.pallas_call(
        functools.partial(_decode_kernel, key_pos_minor=tuple(key_pos_minor),
                          new_pos_minor=tuple(new_pos_minor), has_bias=has_bias, pages=pages, page=page),
        grid_spec=pltpu.PrefetchScalarGridSpec(
            num_scalar_prefetch=1, grid=(bd, n_pages // pages), in_specs=in_specs,
            out_specs=pl.BlockSpec((1, DEC_ROWS, out_w), lambda b, c, pt: (b, 0, 0)),
            scratch_shapes=scratch),
        out_shape=jax.ShapeDtypeStruct((bd, DEC_ROWS, out_w), F32),
        compiler_params=_params("parallel", "arbitrary"),
        name="decode_fox" if has_bias else "decode_mla",
    )(page_table, *ins)


def _pad_axis(a, axis, size):
    pad = [(0, 0)] * a.ndim
    pad[axis] = (0, size - a.shape[axis])
    return jnp.pad(a, pad)


def _conv_proj_kernel(x_ref, g_ref, wa_ref, wb_ref, wz_ref, u_ref, z_ref):
    h = _rms(x_ref[...], g_ref[...]).astype(BF16)
    u_ref[...] = _dot(h, wa_ref[...]) * jax.nn.sigmoid(_dot(h, wb_ref[...]))
    z_ref[...] = _dot(h, wz_ref[...]).astype(BF16)


def _conv_proj(x, g, w):
    rows, d = x.shape
    tm = min(ROW_TILE, rows)
    wa, wb, wz = w
    ch = wa.shape[1]
    return pl.pallas_call(
        _conv_proj_kernel, grid=(rows // tm,),
        in_specs=[pl.BlockSpec((tm, d), lambda i: (i, 0)), _full((1, d)), _full(wa.shape), _full(wb.shape),
                  _full(wz.shape)],
        out_specs=[pl.BlockSpec((tm, ch), lambda i: (i, 0)), pl.BlockSpec((tm, ch), lambda i: (i, 0))],
        out_shape=[jax.ShapeDtypeStruct((rows, ch), F32), jax.ShapeDtypeStruct((rows, ch), BF16)],
        compiler_params=_params("parallel"), name="conv_proj",
    )(x, g.reshape(1, d), wa, wb, wz)


def _ln_silu(c, g, b):
    cc = c - jnp.mean(c, axis=-1, keepdims=True)
    var = jnp.mean(cc * cc, axis=-1, keepdims=True)
    return _silu(cc * lax.rsqrt(var + NORM_EPS) * g + b)


def _conv_prompt_kernel(u_ref, halo_ref, dw_ref, dwb_ref, lng_ref, lnb_ref, c_ref, ext_ref, *, tm):
    first = pl.program_id(1) == 0
    ext_ref[0:CONV_HALO, :] = jnp.where(first, 0.0, halo_ref[0])
    ext_ref[CONV_HALO:, :] = u_ref[0]
    base = CONV_HALO - (CONV_TAPS - 1)
    for r0 in range(0, tm, CONV_CHUNK):
        acc = jnp.zeros((CONV_CHUNK, u_ref.shape[2]), F32)
        for j in range(CONV_TAPS):
            acc = acc + dw_ref[j:j + 1, :] * ext_ref[r0 + base + j:r0 + base + j + CONV_CHUNK, :]
        c = _ln_silu(acc + dwb_ref[...], lng_ref[...], lnb_ref[...])
        c_ref[r0:r0 + CONV_CHUNK, :] = c.astype(BF16)


def _conv_prompt(u, dw, dw_b, ln_g, ln_b):
    batch, seq, ch = u.shape
    tm = min(CONV_ROW_TILE, seq)
    nt = seq // tm
    hpt = tm // CONV_HALO
    return pl.pallas_call(
        functools.partial(_conv_prompt_kernel, tm=tm), grid=(batch, nt),
        in_specs=[pl.BlockSpec((1, tm, ch), lambda b, i: (b, i, 0)),
                  pl.BlockSpec((1, CONV_HALO, ch), lambda b, i: (b, jnp.maximum(i * hpt - 1, 0), 0)),
                  _full(dw.shape), _full((1, ch)), _full((1, ch)), _full((1, ch))],
        out_specs=pl.BlockSpec((tm, ch), lambda b, i: (b * nt + i, 0)),
        out_shape=jax.ShapeDtypeStruct((batch * seq, ch), BF16),
        scratch_shapes=[pltpu.VMEM((tm + CONV_HALO, ch), F32)],
        compiler_params=_params("parallel", "arbitrary"), name="conv_prompt",
    )(u, u, dw, dw_b.reshape(1, ch), ln_g.reshape(1, ch), ln_b.reshape(1, ch))


def _conv_sample_kernel(st_ref, u_ref, dw_ref, dwb_ref, lng_ref, lnb_ref, c_ref, so_ref):
    n_state, t_new = st_ref.shape[0], u_ref.shape[0]

    def ext(i):
        return st_ref[i] if i < n_state else u_ref[i - n_state]

    for i in range(n_state):
        so_ref[i] = ext(i + t_new)
    for t in range(t_new):
        acc = dw_ref[0:1, :] * ext(t)
        for j in range(1, CONV_TAPS):
            acc = acc + dw_ref[j:j + 1, :] * ext(t + j)
        c_ref[t] = _ln_silu(acc + dwb_ref[...], lng_ref[...], lnb_ref[...])


def _conv_sample(state, u_new, dw, dw_b, ln_g, ln_b):
    n_state, bd, ch = state.shape
    t_new = u_new.shape[0]
    sb = min(CONV_SEQ_BLOCK, bd)
    return pl.pallas_call(
        _conv_sample_kernel, grid=(bd // sb,),
        in_specs=[pl.BlockSpec((n_state, sb, ch), lambda i: (0, i, 0)),
                  pl.BlockSpec((t_new, sb, ch), lambda i: (0, i, 0)),
                  _full(dw.shape), _full((1, ch)), _full((1, ch)), _full((1, ch))],
        out_specs=[pl.BlockSpec((t_new, sb, ch), lambda i: (0, i, 0)),
                   pl.BlockSpec((n_state, sb, ch), lambda i: (0, i, 0))],
        out_shape=[jax.ShapeDtypeStruct((t_new, bd, ch), F32), jax.ShapeDtypeStruct((n_state, bd, ch), F32)],
        compiler_params=_params("parallel"), name="conv_sample",
    )(state, u_new, dw, dw_b.reshape(1, ch), ln_g.reshape(1, ch), ln_b.reshape(1, ch))


def _rope_lane_group(x, c, sa, sb):
    return x * c + pltpu.roll(x, V7X_LANES - MLA_ROPE // 2, 1) * sa + pltpu.roll(x, MLA_ROPE // 2, 1) * sb


def _mla_proj_kernel(x_ref, g_ref, wcq_ref, wckv_ref, wkpe_ref, wz_ref, qn_ref, kvn_ref, wuq_ref,
                     c_ref, sa_ref, sb_ref, *rest, prompt):
    if prompt:
        wukn_ref, wuv_ref, q_ref, ckv_ref, kpe_ref, z_ref, kh_ref, vh_ref = rest
    else:
        q_ref, ckv_ref, kpe_ref, z_ref = rest
    scale = LOG2E * (HEAD_DIM + MLA_ROPE) ** -0.5
    c, sa, sb = c_ref[...], sa_ref[...], sb_ref[...]
    h = _rms(x_ref[...], g_ref[...]).astype(BF16)
    cq = _rms(_dot(h, wcq_ref[...]), qn_ref[...]).astype(BF16)
    ckv = _rms(_dot(h, wckv_ref[...]), kvn_ref[...])
    ckv_ref[...] = ckv
    kp = _rope_lane_group(_dot(h, wkpe_ref[...]), c, sa, sb)
    kpe_ref[...] = kp[:, HEAD_DIM:HEAD_DIM + MLA_ROPE]
    z_ref[...] = _dot(h, wz_ref[...]).astype(BF16)
    q = _dot(cq, wuq_ref[...])
    for hd in range(MLA_HEADS):
        sl = slice(hd * MLA_QK_PAD, (hd + 1) * MLA_QK_PAD)
        q_ref[:, sl] = (_rope_lane_group(q[:, sl], c, sa, sb) * scale).astype(BF16)
    if prompt:
        ckv_b = ckv.astype(BF16)
        kn = _dot(ckv_b, wukn_ref[...])
        v = _dot(ckv_b, wuv_ref[...])
        for hd in range(MLA_HEADS):
            kh_ref[0, hd] = (kn[:, hd * MLA_QK_PAD:(hd + 1) * MLA_QK_PAD] + kp).astype(BF16)
            vh_ref[0, hd] = v[:, hd * HEAD_DIM:(hd + 1) * HEAD_DIM].astype(BF16)


def _mla_proj(x, g, w, q_norm, kv_norm, tables, *, batch=None):
    rows, d = x.shape
    prompt = batch is not None
    tm = min(ROW_TILE // 2, rows)
    nt = rows // tm
    wcq, wckv, wkpe, wz, wuq, wukn, wuv = w
    tab_rows = tables[0].shape[0]
    tpt = tab_rows // tm
    ins = [x, g.reshape(1, d), wcq, wckv, wkpe, wz, q_norm.reshape(1, -1), kv_norm.reshape(1, -1), wuq,
           *tables]
    in_specs = [pl.BlockSpec((tm, d), lambda i: (i, 0)), _full((1, d)), _full(wcq.shape), _full(wckv.shape),
                _full(wkpe.shape), _full(wz.shape), _full((1, wcq.shape[1])), _full((1, wckv.shape[1])),
                _full(wuq.shape)]
    in_specs += [pl.BlockSpec((tm, V7X_LANES), lambda i: (i % tpt, 0)) for _ in tables]
    qw = wuq.shape[1]
    out_shape = [jax.ShapeDtypeStruct((rows, qw), BF16), jax.ShapeDtypeStruct((rows, MLA_KV_LORA), F32),
                 jax.ShapeDtypeStruct((rows, MLA_ROPE), F32), jax.ShapeDtypeStruct((rows, wz.shape[1]), BF16)]
    out_specs = [pl.BlockSpec((tm, qw), lambda i: (i, 0)), pl.BlockSpec((tm, MLA_KV_LORA), lambda i: (i, 0)),
                 pl.BlockSpec((tm, MLA_ROPE), lambda i: (i, 0)), pl.BlockSpec((tm, wz.shape[1]), lambda i: (i, 0))]
    if prompt:
        seq = rows // batch
        tps = seq // tm
        ins += [wukn, wuv]
        in_specs += [_full(wukn.shape), _full(wuv.shape)]
        out_shape += [jax.ShapeDtypeStruct((batch, MLA_HEADS, seq, MLA_QK_PAD), BF16),
                      jax.ShapeDtypeStruct((batch, MLA_HEADS, seq, HEAD_DIM), BF16)]
        out_specs += [pl.BlockSpec((1, MLA_HEADS, tm, MLA_QK_PAD), lambda i: (i // tps, 0, i % tps, 0)),
                      pl.BlockSpec((1, MLA_HEADS, tm, HEAD_DIM), lambda i: (i // tps, 0, i % tps, 0))]
    return pl.pallas_call(
        functools.partial(_mla_proj_kernel, prompt=prompt), grid=(nt,), in_specs=in_specs,
        out_specs=out_specs, out_shape=out_shape, compiler_params=_params("parallel"),
        name="mla_proj_prompt" if prompt else "mla_proj_sample",
    )(*ins)


def _rope_tables(positions):
    half = MLA_ROPE // 2
    inv = ROPE_THETA ** (-np.arange(half, dtype=np.float32) / half)
    ang = np.asarray(positions, np.float32)[:, None] * inv[None, :]
    cos, sin = np.cos(ang).astype(np.float32), np.sin(ang).astype(np.float32)
    n = ang.shape[0]
    c = np.zeros((n, V7X_LANES), np.float32)
    sa = np.zeros((n, V7X_LANES), np.float32)
    sb = np.zeros((n, V7X_LANES), np.float32)
    c[:, :HEAD_DIM] = 1.0
    c[:, HEAD_DIM:HEAD_DIM + half] = cos
    c[:, HEAD_DIM + half:HEAD_DIM + 2 * half] = cos
    sa[:, HEAD_DIM:HEAD_DIM + half] = -sin
    sb[:, HEAD_DIM + half:HEAD_DIM + 2 * half] = sin
    return jnp.asarray(c), jnp.asarray(sa), jnp.asarray(sb)


def _headwise_kernel(x_ref, w_ref, o_ref):
    o_ref[0] = _dot(x_ref[0], w_ref[0]).astype(o_ref.dtype)


def _headwise_matmul(x, w, dtype):
    nh, rows, k = x.shape
    n = w.shape[2]
    return pl.pallas_call(
        _headwise_kernel, grid=(nh,),
        in_specs=[pl.BlockSpec((1, rows, k), lambda i: (i, 0, 0)), pl.BlockSpec((1, k, n), lambda i: (i, 0, 0))],
        out_specs=pl.BlockSpec((1, rows, n), lambda i: (i, 0, 0)),
        out_shape=jax.ShapeDtypeStruct((nh, rows, n), dtype),
        compiler_params=_params("parallel"), name="headwise_matmul",
    )(x, w)


def _fox_weights(w_in):
    d = w_in.shape[0]
    width = FOX_HEADS * HEAD_DIM
    kvw = FOX_KV_HEADS * HEAD_DIM
    o1, o2, o3 = width + kvw, width + 2 * kvw, 2 * width + 2 * kvw
    wb = w_in.astype(BF16)
    wf = jnp.zeros((d, V7X_LANES), BF16).at[:, :FOX_HEADS].set(wb[:, o3:])
    return wb[:, :width], wb[:, width:o1], wb[:, o1:o2], wb[:, o2:o3], wf, wb[:, o3:].T


def _fox_layer(xp, xs, cache_k, cache_v, cache_logf, page_table, g_pre, w_in, b_f, w_out, g_post, dims):
    batch, seq, bd, t_new = dims
    w = _fox_weights(w_in)
    w_out = w_out.astype(BF16)
    n_pool, page = cache_k.shape[:2]
    kvw = FOX_KV_HEADS * HEAD_DIM

    q, k, v, z, lft, kh, vh, nb = _fox_proj(xp, g_pre, w, b_f, batch=batch)
    o = _causal_attention(q, kh, vh, nb, heads=FOX_GROUP, shared_kv=True, dk=HEAD_DIM)
    xp = _gated_out(o, z, xp, w_out, g_post)

    qs, ks, vs, zs, lfs = _fox_proj(xs, g_pre, w, b_f)
    own = (np.arange(FOX_HEADS)[:, None] // FOX_GROUP == np.arange(FOX_KV_HEADS)[None, :])
    qbd = (qs.reshape(bd, t_new, FOX_HEADS, 1, HEAD_DIM)
           * jnp.asarray(own, BF16)[None, None, :, :, None]).reshape(bd, t_new * FOX_HEADS, kvw)
    k_pool = cache_k.transpose(0, 2, 3, 1).reshape(n_pool, kvw, page)
    v_pool = cache_v.transpose(0, 2, 3, 1).reshape(n_pool, kvw, page)
    lf_pool = cache_logf.transpose(0, 2, 1)
    os_ = _decode_attention(
        page_table, [qbd], [_pad_axis(ks.reshape(bd, t_new, kvw), 1, V7X_SUBLANES)], [k_pool], (True,), (False,),
        vnew=_pad_axis(vs.reshape(bd, t_new, kvw), 1, V7X_SUBLANES),
        lfnew=_pad_axis(lfs.reshape(bd, t_new, FOX_HEADS).transpose(0, 2, 1), 2, page),
        v_pool=v_pool, lf_pool=lf_pool)
    xs = _gated_out(os_.reshape(bd * t_new, FOX_HEADS * HEAD_DIM), zs, xs, w_out, g_post)

    new_p = (k.reshape(batch, seq, FOX_KV_HEADS, HEAD_DIM), v.reshape(batch, seq, FOX_KV_HEADS, HEAD_DIM),
             lft.transpose(0, 2, 1))
    new_s = (ks.reshape(bd, t_new, FOX_KV_HEADS, HEAD_DIM), vs.reshape(bd, t_new, FOX_KV_HEADS, HEAD_DIM),
             lfs.reshape(bd, t_new, FOX_HEADS))
    return xp, xs, new_p, new_s


def _conv_layer(xp, xs, state, g_pre, w_in, dw, dw_b, ln_g, ln_b, w_out, g_post, dims):
    batch, seq, bd, t_new = dims
    ch = dw.shape[1]
    wb = w_in.astype(BF16)
    w = (wb[:, :ch], wb[:, ch:2 * ch], wb[:, 2 * ch:])
    w_out = w_out.astype(BF16)

    u, z = _conv_proj(xp, g_pre, w)
    u3 = u.reshape(batch, seq, ch)
    c = _conv_prompt(u3, dw, dw_b, ln_g, ln_b)
    xp = _gated_out(c, z, xp, w_out, g_post)

    us, zs = _conv_proj(xs, g_pre, w)
    cs, state_new = _conv_sample(state.transpose(1, 0, 2), us.reshape(bd, t_new, ch).transpose(1, 0, 2),
                                 dw, dw_b, ln_g, ln_b)
    xs = _gated_out(cs.transpose(1, 0, 2).reshape(bd * t_new, ch), zs, xs, w_out, g_post)
    return xp, xs, (u3[:, seq - (CONV_TAPS - 1):],), (state_new.transpose(1, 0, 2),)


def _mla_layer(xp, xs, cache_ckv, cache_kpe, page_table, g_pre, w_in, q_norm, w_uq, kv_norm, w_ukv, w_out,
               g_post, dims):
    batch, seq, bd, t_new = dims
    d = w_in.shape[0]
    page = cache_ckv.shape[1]
    past = page_table.shape[1] * page
    o0, o1, o2 = MLA_Q_LORA, MLA_Q_LORA + MLA_KV_LORA, MLA_Q_LORA + MLA_KV_LORA + MLA_ROPE
    wb = w_in.astype(BF16)
    wkpe = jnp.zeros((d, V7X_LANES), BF16).at[:, HEAD_DIM:HEAD_DIM + MLA_ROPE].set(wb[:, o1:o2])
    dqk = HEAD_DIM + MLA_ROPE
    wuq = jnp.pad(w_uq.astype(BF16).reshape(MLA_Q_LORA, MLA_HEADS, dqk),
                  ((0, 0), (0, 0), (0, MLA_QK_PAD - dqk))).reshape(MLA_Q_LORA, MLA_HEADS * MLA_QK_PAD)
    wukv = w_ukv.astype(BF16).reshape(MLA_KV_LORA, MLA_HEADS, 2 * HEAD_DIM)
    wuk, wuv = wukv[..., :HEAD_DIM], wukv[..., HEAD_DIM:]
    wukn = jnp.pad(wuk, ((0, 0), (0, 0), (0, MLA_QK_PAD - HEAD_DIM))).reshape(MLA_KV_LORA, MLA_HEADS * MLA_QK_PAD)
    w = (wb[:, :o0], wb[:, o0:o1], wkpe, wb[:, o2:], wuq, wukn, wuv.reshape(MLA_KV_LORA, MLA_HEADS * HEAD_DIM))
    w_out = w_out.astype(BF16)

    q, ckv, kpe, z, kh, vh = _mla_proj(xp, g_pre, w, q_norm, kv_norm, _rope_tables(np.arange(seq)), batch=batch)
    o = _causal_attention(q, kh, vh, None, heads=2, shared_kv=False, dk=MLA_QK_PAD)
    xp = _gated_out(o, z, xp, w_out, g_post)

    rows = bd * t_new
    pos_s = np.tile(past + np.arange(t_new), bd)
    qs, ckvs, kpes, zs = _mla_proj(xs, g_pre, w, q_norm, kv_norm, _rope_tables(pos_s))
    q4 = qs.reshape(rows, MLA_HEADS, MLA_QK_PAD)
    q_lat = _headwise_matmul(q4[..., :HEAD_DIM].transpose(1, 0, 2), wuk.transpose(1, 2, 0), BF16)
    q_lat = q_lat.transpose(1, 0, 2).reshape(bd, t_new * MLA_HEADS, MLA_KV_LORA)
    q_pe = q4[..., HEAD_DIM:HEAD_DIM + MLA_ROPE].reshape(bd, t_new * MLA_HEADS, MLA_ROPE)
    o_lat = _decode_attention(
        page_table, [q_lat, q_pe],
        [_pad_axis(ckvs.reshape(bd, t_new, MLA_KV_LORA), 1, V7X_SUBLANES),
         _pad_axis(kpes.reshape(bd, t_new, MLA_ROPE).transpose(0, 2, 1), 2, page)],
        [cache_ckv, cache_kpe.transpose(0, 2, 1)], (False, True), (False, True))
    o_lat = o_lat.astype(BF16).reshape(rows, MLA_HEADS, MLA_KV_LORA).transpose(1, 0, 2)
    os_ = _headwise_matmul(o_lat, wuv.transpose(1, 0, 2), BF16).transpose(1, 0, 2)
    xs = _gated_out(os_.reshape(rows, MLA_HEADS * HEAD_DIM), zs, xs, w_out, g_post)

    new_p = (ckv.reshape(batch, seq, MLA_KV_LORA), kpe.reshape(batch, seq, MLA_ROPE))
    new_s = (ckvs.reshape(bd, t_new, MLA_KV_LORA), kpes.reshape(bd, t_new, MLA_ROPE))
    return xp, xs, new_p, new_s


def kernel(x_prompt, x_sample, cache_k_l0, cache_v_l0, cache_logf_l0, state_conv_l1, cache_ckv_l2, cache_kpe_l2, cache_k_l3, cache_v_l3, cache_logf_l3, page_table, norm_pre_l0, w_in_l0, b_f_l0, w_out_l0, norm_post_l0, norm_pre_l1, w_in_l1, dw_l1, dw_b_l1, ln_g_l1, ln_b_l1, w_out_l1, norm_post_l1, norm_pre_l2, w_in_l2, q_norm_l2, w_uq_l2, kv_norm_l2, w_ukv_l2, w_out_l2, norm_post_l2, norm_pre_l3, w_in_l3, b_f_l3, w_out_l3, norm_post_l3):
    batch, seq, d = x_prompt.shape
    bd, t_new, _ = x_sample.shape
    assert t_new * FOX_HEADS == DEC_ROWS and t_new * MLA_HEADS == DEC_ROWS
    dims = (batch, seq, bd, t_new)
    xp = x_prompt.reshape(batch * seq, d)
    xs = x_sample.reshape(bd * t_new, d)

    xp, xs, (k_p0, v_p0, lf_p0), (k_s0, v_s0, lf_s0) = _fox_layer(
        xp, xs, cache_k_l0, cache_v_l0, cache_logf_l0, page_table,
        norm_pre_l0, w_in_l0, b_f_l0, w_out_l0, norm_post_l0, dims)
    xp, xs, (conv_p1,), (conv_s1,) = _conv_layer(
        xp, xs, state_conv_l1, norm_pre_l1, w_in_l1, dw_l1, dw_b_l1, ln_g_l1, ln_b_l1, w_out_l1,
        norm_post_l1, dims)
    xp, xs, (ckv_p2, kpe_p2), (ckv_s2, kpe_s2) = _mla_layer(
        xp, xs, cache_ckv_l2, cache_kpe_l2, page_table,
        norm_pre_l2, w_in_l2, q_norm_l2, w_uq_l2, kv_norm_l2, w_ukv_l2, w_out_l2, norm_post_l2, dims)
    xp, xs, (k_p3, v_p3, lf_p3), (k_s3, v_s3, lf_s3) = _fox_layer(
        xp, xs, cache_k_l3, cache_v_l3, cache_logf_l3, page_table,
        norm_pre_l3, w_in_l3, b_f_l3, w_out_l3, norm_post_l3, dims)

    return (xp.reshape(batch, seq, d), xs.reshape(bd, t_new, d),
            k_p0, v_p0, lf_p0, k_s0, v_s0, lf_s0,
            conv_p1, conv_s1,
            ckv_p2, kpe_p2, ckv_s2, kpe_s2,
            k_p3, v_p3, lf_p3, k_s3, v_s3, lf_s3)
```

```python
import functools

import numpy as np
import jax
import jax.numpy as jnp
from jax import lax
from jax.experimental import pallas as pl
from jax.experimental.pallas import tpu as pltpu

F32 = jnp.float32
BF16 = jnp.bfloat16

NORM_EPS = 1e-6
LOG2E = 1.4426950408889634
ROPE_THETA = 10000.0
CONV_TAPS = 31

FOX_HEADS = 16
FOX_KV_HEADS = 4
FOX_GROUP = FOX_HEADS // FOX_KV_HEADS
HEAD_DIM = 64
MLA_HEADS = 16
MLA_ROPE = 32
MLA_Q_LORA = 768
MLA_KV_LORA = 256
MLA_QK_PAD = 128

V7X_LANES = 128
V7X_SUBLANES = 8
V7X_VMEM_BYTES = 64 * 1024 * 1024
VMEM_LIMIT = V7X_VMEM_BYTES * 7 // 8

ROW_TILE = 512
ATTN_BLOCK = 512
ATTN_ROW_CHUNK = 512
CONV_ROW_TILE = 256
CONV_CHUNK = 32
CONV_HALO = 32
CONV_SEQ_BLOCK = 32
CUM_GROUP = 256
DEC_ROWS = 64
DEC_PAGES = 16


def _dot(a, b):
    return jnp.dot(a, b, preferred_element_type=F32)


def _dot_nt(a, b):
    return lax.dot_general(a, b, (((1,), (1,)), ((), ())), preferred_element_type=F32)


def _rms(x, g):
    return x * lax.rsqrt(jnp.mean(x * x, axis=-1, keepdims=True) + NORM_EPS) * g


def _log_sigmoid(x):
    return jnp.minimum(x, 0.0) - jnp.log1p(jnp.exp(-jnp.abs(x)))


def _silu(x):
    return x * jax.nn.sigmoid(x)


def _split3(x):
    x1 = x.astype(BF16)
    r = x - x1.astype(F32)
    x2 = r.astype(BF16)
    r = r - x2.astype(F32)
    return x1, x2, r.astype(BF16)


def _upper_tri(n):
    r = lax.broadcasted_iota(jnp.int32, (n, n), 0)
    c = lax.broadcasted_iota(jnp.int32, (n, n), 1)
    return (r <= c).astype(BF16)


def _cumsum_lanes(xt, carry, u=None):
    h, n = xt.shape
    g = min(CUM_GROUP, n)
    ng = n // g
    x = xt if ng == 1 else jnp.concatenate([xt[:, i * g:(i + 1) * g] for i in range(ng)], axis=0)
    if u is None:
        u = _upper_tri(g)
    cum = sum(_dot(p, u) for p in _split3(x))
    if ng > 1:
        r = lax.broadcasted_iota(jnp.int32, (ng * h, ng * h), 0)
        c = lax.broadcasted_iota(jnp.int32, (ng * h, ng * h), 1)
        earlier = ((r % h == c % h) & (c // h < r // h)).astype(BF16)
        tot = jnp.broadcast_to(cum[:, g - 1:g], (ng * h, V7X_LANES))
        off = sum(_dot(earlier, p) for p in _split3(tot))
        cum = cum + jnp.concatenate([off] * (g // V7X_LANES), axis=1)
        carry_rows = jnp.concatenate([carry] * ng, axis=0)
    else:
        carry_rows = carry
    cum = cum + carry_rows
    new_carry = cum[(ng - 1) * h:, g - 1:g]
    if ng > 1:
        cum = jnp.concatenate([cum[i * h:(i + 1) * h] for i in range(ng)], axis=1)
    return cum, new_carry


def _params(*sem):
    return pltpu.CompilerParams(dimension_semantics=sem, vmem_limit_bytes=VMEM_LIMIT)


def _full(shape):
    n = len(shape)
    return pl.BlockSpec(shape, lambda *_: (0,) * n)


def _fox_proj_kernel(x_ref, g_ref, wq_ref, wk_ref, wv_ref, wz_ref, *rest, prompt, tiles_per_seq):
    if prompt:
        (wft_ref, bft_ref, u_ref, q_ref, k_ref, v_ref, z_ref, lft_ref,
         kh_ref, vh_ref, nb_ref, carry_ref) = rest
    else:
        wf_ref, bf_ref, q_ref, k_ref, v_ref, z_ref, lf_ref = rest
    h = _rms(x_ref[...], g_ref[...]).astype(BF16)
    q_ref[...] = (_dot(h, wq_ref[...]) * (LOG2E * HEAD_DIM ** -0.5)).astype(BF16)
    k = _dot(h, wk_ref[...])
    v = _dot(h, wv_ref[...])
    k_ref[...] = k
    v_ref[...] = v
    z_ref[...] = _dot(h, wz_ref[...]).astype(BF16)
    if prompt:
        for hh in range(FOX_KV_HEADS):
            kh_ref[0, hh] = k[:, hh * HEAD_DIM:(hh + 1) * HEAD_DIM].astype(BF16)
            vh_ref[0, hh] = v[:, hh * HEAD_DIM:(hh + 1) * HEAD_DIM].astype(BF16)

        @pl.when(pl.program_id(0) % tiles_per_seq == 0)
        def _():
            carry_ref[...] = jnp.zeros_like(carry_ref)

        lft = _log_sigmoid(_dot_nt(wft_ref[...], h) + bft_ref[...])
        lft_ref[0] = lft
        cum, carry = _cumsum_lanes(lft, carry_ref[:, 0:1], u_ref[...])
        nb_ref[0] = -LOG2E * cum
        carry_ref[...] = jnp.broadcast_to(carry, carry_ref.shape)
    else:
        f = _dot(h, wf_ref[...])[:, :FOX_HEADS] + bf_ref[...]
        lf_ref[...] = _log_sigmoid(f)


def _fox_proj(x, g, w, b_f, *, batch=None):
    rows, d = x.shape
    prompt = batch is not None
    tm = min(ROW_TILE, rows)
    nt = rows // tm
    wq, wk, wv, wz, wf, wft = w
    kvw = wk.shape[1]
    ins = [x, g.reshape(1, d), wq, wk, wv, wz]
    in_specs = [pl.BlockSpec((tm, d), lambda i: (i, 0)), _full((1, d)), _full(wq.shape), _full(wk.shape),
                _full(wv.shape), _full(wz.shape)]
    out_shape = [jax.ShapeDtypeStruct((rows, wq.shape[1]), BF16), jax.ShapeDtypeStruct((rows, kvw), F32),
                 jax.ShapeDtypeStruct((rows, kvw), F32), jax.ShapeDtypeStruct((rows, wz.shape[1]), BF16)]
    out_specs = [pl.BlockSpec((tm, wq.shape[1]), lambda i: (i, 0)), pl.BlockSpec((tm, kvw), lambda i: (i, 0)),
                 pl.BlockSpec((tm, kvw), lambda i: (i, 0)), pl.BlockSpec((tm, wz.shape[1]), lambda i: (i, 0))]
    scratch = []
    tps = 1
    if prompt:
        seq = rows // batch
        tps = seq // tm
        ins += [wft, b_f.reshape(FOX_HEADS, 1), _upper_tri_host(min(CUM_GROUP, tm))]
        in_specs += [_full(wft.shape), _full((FOX_HEADS, 1)), _full((min(CUM_GROUP, tm),) * 2)]
        head_major = jax.ShapeDtypeStruct((batch, FOX_KV_HEADS, seq, HEAD_DIM), BF16)
        per_head = jax.ShapeDtypeStruct((batch, FOX_HEADS, seq), F32)
        out_shape += [per_head, head_major, head_major, per_head]
        hm_spec = pl.BlockSpec((1, FOX_KV_HEADS, tm, HEAD_DIM), lambda i: (i // tps, 0, i % tps, 0))
        ph_spec = pl.BlockSpec((1, FOX_HEADS, tm), lambda i: (i // tps, 0, i % tps))
        out_specs += [ph_spec, hm_spec, hm_spec, ph_spec]
        scratch = [pltpu.VMEM((FOX_HEADS, V7X_LANES), F32)]
    else:
        ins += [wf, b_f.reshape(1, FOX_HEADS)]
        in_specs += [_full(wf.shape), _full((1, FOX_HEADS))]
        out_shape.append(jax.ShapeDtypeStruct((rows, FOX_HEADS), F32))
        out_specs.append(pl.BlockSpec((tm, FOX_HEADS), lambda i: (i, 0)))
    return pl.pallas_call(
        functools.partial(_fox_proj_kernel, prompt=prompt, tiles_per_seq=tps),
        grid=(nt,), in_specs=in_specs, out_specs=out_specs, out_shape=out_shape,
        scratch_shapes=scratch, compiler_params=_params("arbitrary"),
        name="fox_proj_prompt" if prompt else "fox_proj_sample",
    )(*ins)


def _upper_tri_host(n):
    return jnp.asarray(np.triu(np.ones((n, n), np.float32)), dtype=BF16)


def _attn_kernel(*refs, heads, shared_kv, dk, blk, has_bias):
    if has_bias:
        q_ref, k_ref, v_ref, nb_ref, o_ref = refs
    else:
        q_ref, k_ref, v_ref, o_ref = refs
    gi = pl.program_id(1)
    qi = pl.program_id(2)
    rc = min(ATTN_ROW_CHUNK, blk)

    def key_block(j, states):
        diagonal = states is None
        out = []
        off = pl.multiple_of(j * blk, blk)
        if has_bias:
            tile = nb_ref[0, pl.ds(pl.multiple_of(gi * heads // V7X_SUBLANES * V7X_SUBLANES, V7X_SUBLANES),
                                   V7X_SUBLANES), pl.ds(off, blk)]
            nb = jnp.where(gi % (V7X_SUBLANES // heads) == 0, tile[:heads], tile[heads:])
        for g in range(heads):
            kv = 0 if shared_kv else g
            for r0 in range(0, blk, rc):
                kw = r0 + rc if diagonal else blk
                s = _dot_nt(q_ref[r0:r0 + rc, g * dk:(g + 1) * dk], k_ref[0, kv, pl.ds(off, kw), :])
                if has_bias:
                    s = s + nb[g:g + 1, :kw]
                vj = v_ref[0, kv, pl.ds(off, kw), :]
                if diagonal:
                    row = r0 + lax.broadcasted_iota(jnp.int32, s.shape, 0)
                    col = lax.broadcasted_iota(jnp.int32, s.shape, 1)
                    s = jnp.where(row >= col, s, -jnp.inf)
                    m_new = jnp.max(s, axis=1, keepdims=True)
                    p = jnp.exp2(s - m_new)
                    l_new = jnp.sum(p, axis=1, keepdims=True)
                    acc_new = _dot(p.astype(BF16), vj)
                else:
                    m_old, l_old, acc_old = states[len(out)]
                    m_new = jnp.maximum(m_old, jnp.max(s, axis=1, keepdims=True))
                    alpha = jnp.exp2(m_old - m_new)
                    p = jnp.exp2(s - m_new)
                    l_new = alpha * l_old + jnp.sum(p, axis=1, keepdims=True)
                    acc_new = alpha * acc_old + _dot(p.astype(BF16), vj)
                out.append((m_new, l_new, acc_new))
        return tuple(out)

    states = lax.fori_loop(0, qi, key_block, key_block(qi, None))
    for i, (_, l, acc) in enumerate(states):
        g, r0 = divmod(i * rc, blk)
        o_ref[r0:r0 + rc, g * HEAD_DIM:(g + 1) * HEAD_DIM] = (acc / l).astype(BF16)


def _causal_attention(q, kh, vh, nb, *, heads, shared_kv, dk):
    batch, nkv, seq, _ = kh.shape
    kvb = 1 if shared_kv else heads
    groups = nkv // kvb
    blk = min(ATTN_BLOCK, seq)
    nq = seq // blk
    ins = [q, kh, vh]
    in_specs = [pl.BlockSpec((blk, heads * dk), lambda b, g, i: (b * nq + i, g)),
                pl.BlockSpec((1, kvb, seq, dk), lambda b, g, i: (b, g, 0, 0)),
                pl.BlockSpec((1, kvb, seq, HEAD_DIM), lambda b, g, i: (b, g, 0, 0))]
    if nb is not None:
        assert shared_kv and 2 * heads == V7X_SUBLANES
        ins.append(nb)
        in_specs.append(pl.BlockSpec((1, nb.shape[1], seq), lambda b, g, i: (b, 0, 0)))
    return pl.pallas_call(
        functools.partial(_attn_kernel, heads=heads, shared_kv=shared_kv, dk=dk, blk=blk,
                          has_bias=nb is not None),
        grid=(batch, groups, nq), in_specs=in_specs,
        out_specs=pl.BlockSpec((blk, heads * HEAD_DIM), lambda b, g, i: (b * nq + i, g)),
        out_shape=jax.ShapeDtypeStruct((batch * seq, groups * heads * HEAD_DIM), BF16),
        compiler_params=_params("parallel", "parallel", "arbitrary"),
        name="causal_attention_bias" if nb is not None else "causal_attention",
    )(*ins)


def _out_kernel(o_ref, z_ref, x_ref, w_ref, g_ref, y_ref):
    a = (o_ref[...].astype(F32) * _silu(z_ref[...].astype(F32))).astype(BF16)
    y_ref[...] = x_ref[...] + _rms(_dot(a, w_ref[...]), g_ref[...])


def _gated_out(o, z, x, w, g):
    rows, d = x.shape
    tm = min(ROW_TILE, rows)
    width = o.shape[1]
    return pl.pallas_call(
        _out_kernel, grid=(rows // tm,),
        in_specs=[pl.BlockSpec((tm, width), lambda i: (i, 0)), pl.BlockSpec((tm, width), lambda i: (i, 0)),
                  pl.BlockSpec((tm, d), lambda i: (i, 0)), _full(w.shape), _full((1, d))],
        out_specs=pl.BlockSpec((tm, d), lambda i: (i, 0)),
        out_shape=jax.ShapeDtypeStruct((rows, d), F32),
        compiler_params=_params("parallel"), name="gated_out",
    )(o, z, x, w, g.reshape(1, d))


def _decode_kernel(pt_ref, *refs, key_pos_minor, new_pos_minor, has_bias, pages, page, n_chunks):
    it = iter(refs)
    n_parts = len(key_pos_minor)
    n_pools = n_parts + (2 if has_bias else 0)
    q_refs = [next(it) for _ in range(n_parts)]
    knew_refs = [next(it) for _ in range(n_parts)]
    vnew_ref = next(it) if has_bias else None
    lfnew_ref = next(it) if has_bias else None
    pool_refs = [next(it) for _ in range(n_pools)]
    o_ref = next(it)
    bufs = [next(it) for _ in range(n_pools)]
    sem = next(it)
    b = pl.program_id(0)

    def copies(seq, chunk, slot):
        return [pltpu.make_async_copy(pool.at[pt_ref[seq, chunk * pages + j]], buf.at[slot, j],
                                      sem.at[slot, i])
                for i, (pool, buf) in enumerate(zip(pool_refs, bufs)) for j in range(pages)]

    @pl.when(b == 0)
    def _():
        for cp in copies(0, 0, 0):
            cp.start()

    qs = [r[0] for r in q_refs]

    def scores(keys, pos_minor):
        return sum((_dot if pm else _dot_nt)(q, k) for q, k, pm in zip(qs, keys, pos_minor))

    def update(state, s, values, values_pos_minor):
        m, l, acc = state
        m_new = jnp.maximum(m, jnp.max(s, axis=1, keepdims=True))
        alpha = jnp.exp2(m - m_new)
        p = jnp.exp2(s - m_new)
        l = alpha * l + jnp.sum(p, axis=1, keepdims=True)
        pb = p.astype(BF16)
        pv = sum((_dot_nt if values_pos_minor else _dot)(pb[:, j * page:(j + 1) * page], vj)
                 for j, vj in enumerate(values))
        return m_new, l, alpha * acc + pv

    def tile_rows(nb):
        return jnp.concatenate([nb] * (DEC_ROWS // FOX_HEADS), axis=0)

    dv = bufs[n_parts].shape[2] if has_bias else bufs[0].shape[3]
    state = (jnp.full((DEC_ROWS, 1), -jnp.inf, F32), jnp.zeros((DEC_ROWS, 1), F32),
             jnp.zeros((DEC_ROWS, dv), F32))
    carry = jnp.zeros((FOX_HEADS, 1), F32)
    for c in range(n_chunks):
        slot = c % 2
        if c + 1 < n_chunks:
            for cp in copies(b, c + 1, 1 - slot):
                cp.start()
        else:
            @pl.when(b + 1 < pl.num_programs(0))
            def _():
                for cp in copies(b + 1, 0, 1 - slot):
                    cp.start()
        for cp in copies(b, c, slot):
            cp.wait()
        kb = [[bufs[i][slot, j].astype(BF16) for j in range(pages)] for i in range(n_parts)]
        s = jnp.concatenate([scores([kb[i][j] for i in range(n_parts)], key_pos_minor) for j in range(pages)],
                            axis=1)
        if has_bias:
            lf = jnp.concatenate([bufs[n_parts + 1][slot, j] for j in range(pages)], axis=1)
            cum, carry = _cumsum_lanes(lf, carry)
            s = s - tile_rows(LOG2E * cum)
            values = [bufs[n_parts][slot, j].astype(BF16) for j in range(pages)]
        else:
            values = kb[0]
        state = update(state, s, values, has_bias)

    def pad(a):
        return jnp.concatenate([a, jnp.zeros((page - a.shape[0], a.shape[1]), a.dtype)], axis=0)

    knew = [(r[0] if pm else pad(r[0])).astype(BF16) for r, pm in zip(knew_refs, new_pos_minor)]
    sn = scores(knew, new_pos_minor)
    if has_bias:
        cum_new, _ = _cumsum_lanes(lfnew_ref[0], carry)
        sn = sn - tile_rows(LOG2E * cum_new)
        vnew = pad(vnew_ref[0]).astype(BF16)
    else:
        vnew = knew[0]
    tok = lax.broadcasted_iota(jnp.int32, sn.shape, 0) // FOX_HEADS
    new = lax.broadcasted_iota(jnp.int32, sn.shape, 1)
    sn = jnp.where(new <= tok, sn, -jnp.inf)
    _, l, acc = update(state, sn, [vnew], False)
    o = acc / l
    if has_bias:
        kvh = (lax.broadcasted_iota(jnp.int32, (DEC_ROWS, HEAD_DIM), 0) % FOX_HEADS) // FOX_GROUP
        o = sum(jnp.where(kvh == k, o[:, k * HEAD_DIM:(k + 1) * HEAD_DIM], 0.0)
                for k in range(FOX_KV_HEADS))
    o_ref[0] = o


def _decode_attention(page_table, qs, news, pools, key_pos_minor, new_pos_minor, *,
                      vnew=None, lfnew=None, v_pool=None, lf_pool=None):
    bd, n_pages = page_table.shape
    has_bias = lf_pool is not None
    pages = DEC_PAGES if n_pages % DEC_PAGES == 0 else 4
    page = pools[0].shape[2 if key_pos_minor[0] else 1]
    dv = v_pool.shape[1] if has_bias else pools[0].shape[2]

    n_chunks = n_pages // pages
    assert n_chunks % 2 == 0

    def seq_spec(a):
        return pl.BlockSpec((1,) + a.shape[1:], lambda b, pt: (b, 0, 0))

    ins = list(qs) + list(news)
    if has_bias:
        ins += [vnew, lfnew]
    in_specs = [seq_spec(a) for a in ins]
    all_pools = list(pools) + ([v_pool, lf_pool] if has_bias else [])
    ins += all_pools
    in_specs += [pl.BlockSpec(memory_space=pl.ANY)] * len(all_pools)
    scratch = [pltpu.VMEM((2, pages) + pool.shape[1:], F32) for pool in all_pools]
    scratch.append(pltpu.SemaphoreType.DMA((2, len(all_pools))))
    out_w = HEAD_DIM if has_bias else dv
    return pl.pallas_call(
        functools.partial(_decode_kernel, key_pos_minor=tuple(key_pos_minor),
                          new_pos_minor=tuple(new_pos_minor), has_bias=has_bias, pages=pages, page=page,
                          n_chunks=n_chunks),
        grid_spec=pltpu.PrefetchScalarGridSpec(
            num_scalar_prefetch=1, grid=(bd,), in_specs=in_specs,
            out_specs=pl.BlockSpec((1, DEC_ROWS, out_w), lambda b, pt: (b, 0, 0)),
            scratch_shapes=scratch),
        out_shape=jax.ShapeDtypeStruct((bd, DEC_ROWS, out_w), F32),
        compiler_params=_params("arbitrary"),
        name="decode_fox" if has_bias else "decode_mla",
    )(page_table, *ins)


def _pad_axis(a, axis, size):
    pad = [(0, 0)] * a.ndim
    pad[axis] = (0, size - a.shape[axis])
    return jnp.pad(a, pad)


def _conv_proj_kernel(x_ref, g_ref, wa_ref, wb_ref, wz_ref, u_ref, z_ref):
    h = _rms(x_ref[...], g_ref[...]).astype(BF16)
    u_ref[...] = _dot(h, wa_ref[...]) * jax.nn.sigmoid(_dot(h, wb_ref[...]))
    z_ref[...] = _dot(h, wz_ref[...]).astype(BF16)


def _conv_proj(x, g, w):
    rows, d = x.shape
    tm = min(ROW_TILE, rows)
    wa, wb, wz = w
    ch = wa.shape[1]
    return pl.pallas_call(
        _conv_proj_kernel, grid=(rows // tm,),
        in_specs=[pl.BlockSpec((tm, d), lambda i: (i, 0)), _full((1, d)), _full(wa.shape), _full(wb.shape),
                  _full(wz.shape)],
        out_specs=[pl.BlockSpec((tm, ch), lambda i: (i, 0)), pl.BlockSpec((tm, ch), lambda i: (i, 0))],
        out_shape=[jax.ShapeDtypeStruct((rows, ch), F32), jax.ShapeDtypeStruct((rows, ch), BF16)],
        compiler_params=_params("parallel"), name="conv_proj",
    )(x, g.reshape(1, d), wa, wb, wz)


def _ln_silu(c, g, b):
    cc = c - jnp.mean(c, axis=-1, keepdims=True)
    var = jnp.mean(cc * cc, axis=-1, keepdims=True)
    return _silu(cc * lax.rsqrt(var + NORM_EPS) * g + b)


def _conv_prompt_kernel(u_ref, halo_ref, dw_ref, dwb_ref, lng_ref, lnb_ref, c_ref, ext_ref, *, tm):
    first = pl.program_id(1) == 0
    ext_ref[0:CONV_HALO, :] = jnp.where(first, 0.0, halo_ref[0])
    ext_ref[CONV_HALO:, :] = u_ref[0]
    base = CONV_HALO - (CONV_TAPS - 1)
    for r0 in range(0, tm, CONV_CHUNK):
        acc = jnp.zeros((CONV_CHUNK, u_ref.shape[2]), F32)
        for j in range(CONV_TAPS):
            acc = acc + dw_ref[j:j + 1, :] * ext_ref[r0 + base + j:r0 + base + j + CONV_CHUNK, :]
        c = _ln_silu(acc + dwb_ref[...], lng_ref[...], lnb_ref[...])
        c_ref[r0:r0 + CONV_CHUNK, :] = c.astype(BF16)


def _conv_prompt(u, dw, dw_b, ln_g, ln_b):
    batch, seq, ch = u.shape
    tm = min(CONV_ROW_TILE, seq)
    nt = seq // tm
    hpt = tm // CONV_HALO
    return pl.pallas_call(
        functools.partial(_conv_prompt_kernel, tm=tm), grid=(batch, nt),
        in_specs=[pl.BlockSpec((1, tm, ch), lambda b, i: (b, i, 0)),
                  pl.BlockSpec((1, CONV_HALO, ch), lambda b, i: (b, jnp.maximum(i * hpt - 1, 0), 0)),
                  _full(dw.shape), _full((1, ch)), _full((1, ch)), _full((1, ch))],
        out_specs=pl.BlockSpec((tm, ch), lambda b, i: (b * nt + i, 0)),
        out_shape=jax.ShapeDtypeStruct((batch * seq, ch), BF16),
        scratch_shapes=[pltpu.VMEM((tm + CONV_HALO, ch), F32)],
        compiler_params=_params("parallel", "arbitrary"), name="conv_prompt",
    )(u, u, dw, dw_b.reshape(1, ch), ln_g.reshape(1, ch), ln_b.reshape(1, ch))


def _conv_sample_kernel(st_ref, u_ref, dw_ref, dwb_ref, lng_ref, lnb_ref, c_ref, so_ref):
    n_state, t_new = st_ref.shape[0], u_ref.shape[0]

    def ext(i):
        return st_ref[i] if i < n_state else u_ref[i - n_state]

    for i in range(n_state):
        so_ref[i] = ext(i + t_new)
    for t in range(t_new):
        acc = dw_ref[0:1, :] * ext(t)
        for j in range(1, CONV_TAPS):
            acc = acc + dw_ref[j:j + 1, :] * ext(t + j)
        c_ref[t] = _ln_silu(acc + dwb_ref[...], lng_ref[...], lnb_ref[...])


def _conv_sample(state, u_new, dw, dw_b, ln_g, ln_b):
    n_state, bd, ch = state.shape
    t_new = u_new.shape[0]
    sb = min(CONV_SEQ_BLOCK, bd)
    return pl.pallas_call(
        _conv_sample_kernel, grid=(bd // sb,),
        in_specs=[pl.BlockSpec((n_state, sb, ch), lambda i: (0, i, 0)),
                  pl.BlockSpec((t_new, sb, ch), lambda i: (0, i, 0)),
                  _full(dw.shape), _full((1, ch)), _full((1, ch)), _full((1, ch))],
        out_specs=[pl.BlockSpec((t_new, sb, ch), lambda i: (0, i, 0)),
                   pl.BlockSpec((n_state, sb, ch), lambda i: (0, i, 0))],
        out_shape=[jax.ShapeDtypeStruct((t_new, bd, ch), F32), jax.ShapeDtypeStruct((n_state, bd, ch), F32)],
        compiler_params=_params("parallel"), name="conv_sample",
    )(state, u_new, dw, dw_b.reshape(1, ch), ln_g.reshape(1, ch), ln_b.reshape(1, ch))


def _rope_lane_group(x, c, sa, sb):
    return x * c + pltpu.roll(x, V7X_LANES - MLA_ROPE // 2, 1) * sa + pltpu.roll(x, MLA_ROPE // 2, 1) * sb


def _mla_proj_kernel(x_ref, g_ref, wcq_ref, wckv_ref, wkpe_ref, wz_ref, qn_ref, kvn_ref, wuq_ref,
                     c_ref, sa_ref, sb_ref, *rest, prompt):
    if prompt:
        wukn_ref, wuv_ref, q_ref, ckv_ref, kpe_ref, z_ref, kh_ref, vh_ref = rest
    else:
        q_ref, ckv_ref, kpe_ref, z_ref = rest
    scale = LOG2E * (HEAD_DIM + MLA_ROPE) ** -0.5
    c, sa, sb = c_ref[...], sa_ref[...], sb_ref[...]
    h = _rms(x_ref[...], g_ref[...]).astype(BF16)
    cq = _rms(_dot(h, wcq_ref[...]), qn_ref[...]).astype(BF16)
    ckv = _rms(_dot(h, wckv_ref[...]), kvn_ref[...])
    ckv_ref[...] = ckv
    kp = _rope_lane_group(_dot(h, wkpe_ref[...]), c, sa, sb)
    kpe_ref[...] = kp[:, HEAD_DIM:HEAD_DIM + MLA_ROPE]
    z_ref[...] = _dot(h, wz_ref[...]).astype(BF16)
    q = _dot(cq, wuq_ref[...])
    for hd in range(MLA_HEADS):
        sl = slice(hd * MLA_QK_PAD, (hd + 1) * MLA_QK_PAD)
        q_ref[:, sl] = (_rope_lane_group(q[:, sl], c, sa, sb) * scale).astype(BF16)
    if prompt:
        ckv_b = ckv.astype(BF16)
        kn = _dot(ckv_b, wukn_ref[...])
        v = _dot(ckv_b, wuv_ref[...])
        for hd in range(MLA_HEADS):
            kh_ref[0, hd] = (kn[:, hd * MLA_QK_PAD:(hd + 1) * MLA_QK_PAD] + kp).astype(BF16)
            vh_ref[0, hd] = v[:, hd * HEAD_DIM:(hd + 1) * HEAD_DIM].astype(BF16)


def _mla_proj(x, g, w, q_norm, kv_norm, tables, *, batch=None):
    rows, d = x.shape
    prompt = batch is not None
    tm = min(ROW_TILE // 2, rows)
    nt = rows // tm
    wcq, wckv, wkpe, wz, wuq, wukn, wuv = w
    tab_rows = tables[0].shape[0]
    tpt = tab_rows // tm
    ins = [x, g.reshape(1, d), wcq, wckv, wkpe, wz, q_norm.reshape(1, -1), kv_norm.reshape(1, -1), wuq,
           *tables]
    in_specs = [pl.BlockSpec((tm, d), lambda i: (i, 0)), _full((1, d)), _full(wcq.shape), _full(wckv.shape),
                _full(wkpe.shape), _full(wz.shape), _full((1, wcq.shape[1])), _full((1, wckv.shape[1])),
                _full(wuq.shape)]
    in_specs += [pl.BlockSpec((tm, V7X_LANES), lambda i: (i % tpt, 0)) for _ in tables]
    qw = wuq.shape[1]
    out_shape = [jax.ShapeDtypeStruct((rows, qw), BF16), jax.ShapeDtypeStruct((rows, MLA_KV_LORA), F32),
                 jax.ShapeDtypeStruct((rows, MLA_ROPE), F32), jax.ShapeDtypeStruct((rows, wz.shape[1]), BF16)]
    out_specs = [pl.BlockSpec((tm, qw), lambda i: (i, 0)), pl.BlockSpec((tm, MLA_KV_LORA), lambda i: (i, 0)),
                 pl.BlockSpec((tm, MLA_ROPE), lambda i: (i, 0)), pl.BlockSpec((tm, wz.shape[1]), lambda i: (i, 0))]
    if prompt:
        seq = rows // batch
        tps = seq // tm
        ins += [wukn, wuv]
        in_specs += [_full(wukn.shape), _full(wuv.shape)]
        out_shape += [jax.ShapeDtypeStruct((batch, MLA_HEADS, seq, MLA_QK_PAD), BF16),
                      jax.ShapeDtypeStruct((batch, MLA_HEADS, seq, HEAD_DIM), BF16)]
        out_specs += [pl.BlockSpec((1, MLA_HEADS, tm, MLA_QK_PAD), lambda i: (i // tps, 0, i % tps, 0)),
                      pl.BlockSpec((1, MLA_HEADS, tm, HEAD_DIM), lambda i: (i // tps, 0, i % tps, 0))]
    return pl.pallas_call(
        functools.partial(_mla_proj_kernel, prompt=prompt), grid=(nt,), in_specs=in_specs,
        out_specs=out_specs, out_shape=out_shape, compiler_params=_params("parallel"),
        name="mla_proj_prompt" if prompt else "mla_proj_sample",
    )(*ins)


def _rope_tables(positions):
    half = MLA_ROPE // 2
    inv = ROPE_THETA ** (-np.arange(half, dtype=np.float32) / half)
    ang = np.asarray(positions, np.float32)[:, None] * inv[None, :]
    cos, sin = np.cos(ang).astype(np.float32), np.sin(ang).astype(np.float32)
    n = ang.shape[0]
    c = np.zeros((n, V7X_LANES), np.float32)
    sa = np.zeros((n, V7X_LANES), np.float32)
    sb = np.zeros((n, V7X_LANES), np.float32)
    c[:, :HEAD_DIM] = 1.0
    c[:, HEAD_DIM:HEAD_DIM + half] = cos
    c[:, HEAD_DIM + half:HEAD_DIM + 2 * half] = cos
    sa[:, HEAD_DIM:HEAD_DIM + half] = -sin
    sb[:, HEAD_DIM + half:HEAD_DIM + 2 * half] = sin
    return jnp.asarray(c), jnp.asarray(sa), jnp.asarray(sb)


def _headwise_kernel(x_ref, w_ref, o_ref):
    o_ref[0] = _dot(x_ref[0], w_ref[0]).astype(o_ref.dtype)


def _headwise_matmul(x, w, dtype):
    nh, rows, k = x.shape
    n = w.shape[2]
    return pl.pallas_call(
        _headwise_kernel, grid=(nh,),
        in_specs=[pl.BlockSpec((1, rows, k), lambda i: (i, 0, 0)), pl.BlockSpec((1, k, n), lambda i: (i, 0, 0))],
        out_specs=pl.BlockSpec((1, rows, n), lambda i: (i, 0, 0)),
        out_shape=jax.ShapeDtypeStruct((nh, rows, n), dtype),
        compiler_params=_params("parallel"), name="headwise_matmul",
    )(x, w)


def _fox_weights(w_in):
    d = w_in.shape[0]
    width = FOX_HEADS * HEAD_DIM
    kvw = FOX_KV_HEADS * HEAD_DIM
    o1, o2, o3 = width + kvw, width + 2 * kvw, 2 * width + 2 * kvw
    wb = w_in.astype(BF16)
    wf = jnp.zeros((d, V7X_LANES), BF16).at[:, :FOX_HEADS].set(wb[:, o3:])
    return wb[:, :width], wb[:, width:o1], wb[:, o1:o2], wb[:, o2:o3], wf, wb[:, o3:].T


def _fox_layer(xp, xs, cache_k, cache_v, cache_logf, page_table, g_pre, w_in, b_f, w_out, g_post, dims):
    batch, seq, bd, t_new = dims
    w = _fox_weights(w_in)
    w_out = w_out.astype(BF16)
    n_pool, page = cache_k.shape[:2]
    kvw = FOX_KV_HEADS * HEAD_DIM

    q, k, v, z, lft, kh, vh, nb = _fox_proj(xp, g_pre, w, b_f, batch=batch)
    o = _causal_attention(q, kh, vh, nb, heads=FOX_GROUP, shared_kv=True, dk=HEAD_DIM)
    xp = _gated_out(o, z, xp, w_out, g_post)

    qs, ks, vs, zs, lfs = _fox_proj(xs, g_pre, w, b_f)
    own = (np.arange(FOX_HEADS)[:, None] // FOX_GROUP == np.arange(FOX_KV_HEADS)[None, :])
    qbd = (qs.reshape(bd, t_new, FOX_HEADS, 1, HEAD_DIM)
           * jnp.asarray(own, BF16)[None, None, :, :, None]).reshape(bd, t_new * FOX_HEADS, kvw)
    k_pool = cache_k.transpose(0, 2, 3, 1).reshape(n_pool, kvw, page)
    v_pool = cache_v.transpose(0, 2, 3, 1).reshape(n_pool, kvw, page)
    lf_pool = cache_logf.transpose(0, 2, 1)
    os_ = _decode_attention(
        page_table, [qbd], [_pad_axis(ks.reshape(bd, t_new, kvw), 1, V7X_SUBLANES)], [k_pool], (True,), (False,),
        vnew=_pad_axis(vs.reshape(bd, t_new, kvw), 1, V7X_SUBLANES),
        lfnew=_pad_axis(lfs.reshape(bd, t_new, FOX_HEADS).transpose(0, 2, 1), 2, page),
        v_pool=v_pool, lf_pool=lf_pool)
    xs = _gated_out(os_.reshape(bd * t_new, FOX_HEADS * HEAD_DIM), zs, xs, w_out, g_post)

    new_p = (k.reshape(batch, seq, FOX_KV_HEADS, HEAD_DIM), v.reshape(batch, seq, FOX_KV_HEADS, HEAD_DIM),
             lft.transpose(0, 2, 1))
    new_s = (ks.reshape(bd, t_new, FOX_KV_HEADS, HEAD_DIM), vs.reshape(bd, t_new, FOX_KV_HEADS, HEAD_DIM),
             lfs.reshape(bd, t_new, FOX_HEADS))
    return xp, xs, new_p, new_s


def _conv_layer(xp, xs, state, g_pre, w_in, dw, dw_b, ln_g, ln_b, w_out, g_post, dims):
    batch, seq, bd, t_new = dims
    ch = dw.shape[1]
    wb = w_in.astype(BF16)
    w = (wb[:, :ch], wb[:, ch:2 * ch], wb[:, 2 * ch:])
    w_out = w_out.astype(BF16)

    u, z = _conv_proj(xp, g_pre, w)
    u3 = u.reshape(batch, seq, ch)
    c = _conv_prompt(u3, dw, dw_b, ln_g, ln_b)
    xp = _gated_out(c, z, xp, w_out, g_post)

    us, zs = _conv_proj(xs, g_pre, w)
    cs, state_new = _conv_sample(state.transpose(1, 0, 2), us.reshape(bd, t_new, ch).transpose(1, 0, 2),
                                 dw, dw_b, ln_g, ln_b)
    xs = _gated_out(cs.transpose(1, 0, 2).reshape(bd * t_new, ch), zs, xs, w_out, g_post)
    return xp, xs, (u3[:, seq - (CONV_TAPS - 1):],), (state_new.transpose(1, 0, 2),)


def _mla_layer(xp, xs, cache_ckv, cache_kpe, page_table, g_pre, w_in, q_norm, w_uq, kv_norm, w_ukv, w_out,
               g_post, dims):
    batch, seq, bd, t_new = dims
    d = w_in.shape[0]
    page = cache_ckv.shape[1]
    past = page_table.shape[1] * page
    o0, o1, o2 = MLA_Q_LORA, MLA_Q_LORA + MLA_KV_LORA, MLA_Q_LORA + MLA_KV_LORA + MLA_ROPE
    wb = w_in.astype(BF16)
    wkpe = jnp.zeros((d, V7X_LANES), BF16).at[:, HEAD_DIM:HEAD_DIM + MLA_ROPE].set(wb[:, o1:o2])
    dqk = HEAD_DIM + MLA_ROPE
    wuq = jnp.pad(w_uq.astype(BF16).reshape(MLA_Q_LORA, MLA_HEADS, dqk),
                  ((0, 0), (0, 0), (0, MLA_QK_PAD - dqk))).reshape(MLA_Q_LORA, MLA_HEADS * MLA_QK_PAD)
    wukv = w_ukv.astype(BF16).reshape(MLA_KV_LORA, MLA_HEADS, 2 * HEAD_DIM)
    wuk, wuv = wukv[..., :HEAD_DIM], wukv[..., HEAD_DIM:]
    wukn = jnp.pad(wuk, ((0, 0), (0, 0), (0, MLA_QK_PAD - HEAD_DIM))).reshape(MLA_KV_LORA, MLA_HEADS * MLA_QK_PAD)
    w = (wb[:, :o0], wb[:, o0:o1], wkpe, wb[:, o2:], wuq, wukn, wuv.reshape(MLA_KV_LORA, MLA_HEADS * HEAD_DIM))
    w_out = w_out.astype(BF16)

    q, ckv, kpe, z, kh, vh = _mla_proj(xp, g_pre, w, q_norm, kv_norm, _rope_tables(np.arange(seq)), batch=batch)
    o = _causal_attention(q, kh, vh, None, heads=2, shared_kv=False, dk=MLA_QK_PAD)
    xp = _gated_out(o, z, xp, w_out, g_post)

    rows = bd * t_new
    pos_s = np.tile(past + np.arange(t_new), bd)
    qs, ckvs, kpes, zs = _mla_proj(xs, g_pre, w, q_norm, kv_norm, _rope_tables(pos_s))
    q4 = qs.reshape(rows, MLA_HEADS, MLA_QK_PAD)
    q_lat = _headwise_matmul(q4[..., :HEAD_DIM].transpose(1, 0, 2), wuk.transpose(1, 2, 0), BF16)
    q_lat = q_lat.transpose(1, 0, 2).reshape(bd, t_new * MLA_HEADS, MLA_KV_LORA)
    q_pe = q4[..., HEAD_DIM:HEAD_DIM + MLA_ROPE].reshape(bd, t_new * MLA_HEADS, MLA_ROPE)
    o_lat = _decode_attention(
        page_table, [q_lat, q_pe],
        [_pad_axis(ckvs.reshape(bd, t_new, MLA_KV_LORA), 1, V7X_SUBLANES),
         _pad_axis(kpes.reshape(bd, t_new, MLA_ROPE).transpose(0, 2, 1), 2, page)],
        [cache_ckv, cache_kpe.transpose(0, 2, 1)], (False, True), (False, True))
    o_lat = o_lat.astype(BF16).reshape(rows, MLA_HEADS, MLA_KV_LORA).transpose(1, 0, 2)
    os_ = _headwise_matmul(o_lat, wuv.transpose(1, 0, 2), BF16).transpose(1, 0, 2)
    xs = _gated_out(os_.reshape(rows, MLA_HEADS * HEAD_DIM), zs, xs, w_out, g_post)

    new_p = (ckv.reshape(batch, seq, MLA_KV_LORA), kpe.reshape(batch, seq, MLA_ROPE))
    new_s = (ckvs.reshape(bd, t_new, MLA_KV_LORA), kpes.reshape(bd, t_new, MLA_ROPE))
    return xp, xs, new_p, new_s


def kernel(x_prompt, x_sample, cache_k_l0, cache_v_l0, cache_logf_l0, state_conv_l1, cache_ckv_l2, cache_kpe_l2, cache_k_l3, cache_v_l3, cache_logf_l3, page_table, norm_pre_l0, w_in_l0, b_f_l0, w_out_l0, norm_post_l0, norm_pre_l1, w_in_l1, dw_l1, dw_b_l1, ln_g_l1, ln_b_l1, w_out_l1, norm_post_l1, norm_pre_l2, w_in_l2, q_norm_l2, w_uq_l2, kv_norm_l2, w_ukv_l2, w_out_l2, norm_post_l2, norm_pre_l3, w_in_l3, b_f_l3, w_out_l3, norm_post_l3):
    batch, seq, d = x_prompt.shape
    bd, t_new, _ = x_sample.shape
    assert t_new * FOX_HEADS == DEC_ROWS and t_new * MLA_HEADS == DEC_ROWS
    dims = (batch, seq, bd, t_new)
    xp = x_prompt.reshape(batch * seq, d)
    xs = x_sample.reshape(bd * t_new, d)

    xp, xs, (k_p0, v_p0, lf_p0), (k_s0, v_s0, lf_s0) = _fox_layer(
        xp, xs, cache_k_l0, cache_v_l0, cache_logf_l0, page_table,
        norm_pre_l0, w_in_l0, b_f_l0, w_out_l0, norm_post_l0, dims)
    xp, xs, (conv_p1,), (conv_s1,) = _conv_layer(
        xp, xs, state_conv_l1, norm_pre_l1, w_in_l1, dw_l1, dw_b_l1, ln_g_l1, ln_b_l1, w_out_l1,
        norm_post_l1, dims)
    xp, xs, (ckv_p2, kpe_p2), (ckv_s2, kpe_s2) = _mla_layer(
        xp, xs, cache_ckv_l2, cache_kpe_l2, page_table,
        norm_pre_l2, w_in_l2, q_norm_l2, w_uq_l2, kv_norm_l2, w_ukv_l2, w_out_l2, norm_post_l2, dims)
    xp, xs, (k_p3, v_p3, lf_p3), (k_s3, v_s3, lf_s3) = _fox_layer(
        xp, xs, cache_k_l3, cache_v_l3, cache_logf_l3, page_table,
        norm_pre_l3, w_in_l3, b_f_l3, w_out_l3, norm_post_l3, dims)

    return (xp.reshape(batch, seq, d), xs.reshape(bd, t_new, d),
            k_p0, v_p0, lf_p0, k_s0, v_s0, lf_s0,
            conv_p1, conv_s1,
            ckv_p2, kpe_p2, ckv_s2, kpe_s2,
            k_p3, v_p3, lf_p3, k_s3, v_s3, lf_s3)
```

```python
import functools

import numpy as np
import jax
import jax.numpy as jnp
from jax import lax
from jax.experimental import pallas as pl
from jax.experimental.pallas import tpu as pltpu

F32 = jnp.float32
BF16 = jnp.bfloat16

NORM_EPS = 1e-6
LOG2E = 1.4426950408889634
ROPE_THETA = 10000.0
CONV_TAPS = 31

FOX_HEADS = 16
FOX_KV_HEADS = 4
FOX_GROUP = FOX_HEADS // FOX_KV_HEADS
HEAD_DIM = 64
MLA_HEADS = 16
MLA_ROPE = 32
MLA_Q_LORA = 768
MLA_KV_LORA = 256
MLA_QK_PAD = 128

V7X_LANES = 128
V7X_SUBLANES = 8
V7X_VMEM_BYTES = 64 * 1024 * 1024
VMEM_LIMIT = V7X_VMEM_BYTES * 7 // 8

ROW_TILE = 512
ATTN_BLOCK = 512
ATTN_ROW_CHUNK = 512
CONV_ROW_TILE = 256
CONV_CHUNK = 32
CONV_HALO = 32
CONV_SEQ_BLOCK = 32
CUM_GROUP = 256
DEC_ROWS = 64
DEC_PAGES = 16
DEC_PAGE_GROUP = 2


def _dot(a, b):
    return jnp.dot(a, b, preferred_element_type=F32)


def _dot_nt(a, b):
    return lax.dot_general(a, b, (((1,), (1,)), ((), ())), preferred_element_type=F32)


def _rms(x, g):
    return x * lax.rsqrt(jnp.mean(x * x, axis=-1, keepdims=True) + NORM_EPS) * g


def _log_sigmoid(x):
    return jnp.minimum(x, 0.0) - jnp.log1p(jnp.exp(-jnp.abs(x)))


def _silu(x):
    return x * jax.nn.sigmoid(x)


def _split3(x):
    x1 = x.astype(BF16)
    r = x - x1.astype(F32)
    x2 = r.astype(BF16)
    r = r - x2.astype(F32)
    return x1, x2, r.astype(BF16)


def _upper_tri(n):
    r = lax.broadcasted_iota(jnp.int32, (n, n), 0)
    c = lax.broadcasted_iota(jnp.int32, (n, n), 1)
    return (r <= c).astype(BF16)


def _cumsum_lanes(xt, carry, u=None):
    h, n = xt.shape
    g = min(CUM_GROUP, n)
    ng = n // g
    x = xt if ng == 1 else jnp.concatenate([xt[:, i * g:(i + 1) * g] for i in range(ng)], axis=0)
    if u is None:
        u = _upper_tri(g)
    cum = sum(_dot(p, u) for p in _split3(x))
    if ng > 1:
        r = lax.broadcasted_iota(jnp.int32, (ng * h, ng * h), 0)
        c = lax.broadcasted_iota(jnp.int32, (ng * h, ng * h), 1)
        earlier = ((r % h == c % h) & (c // h < r // h)).astype(BF16)
        tot = jnp.broadcast_to(cum[:, g - 1:g], (ng * h, V7X_LANES))
        off = sum(_dot(earlier, p) for p in _split3(tot))
        cum = cum + jnp.concatenate([off] * (g // V7X_LANES), axis=1)
        carry_rows = jnp.concatenate([carry] * ng, axis=0)
    else:
        carry_rows = carry
    cum = cum + carry_rows
    new_carry = cum[(ng - 1) * h:, g - 1:g]
    if ng > 1:
        cum = jnp.concatenate([cum[i * h:(i + 1) * h] for i in range(ng)], axis=1)
    return cum, new_carry


def _values_with_ones(v, hd):
    slab = v[:, (hd // 2) * V7X_LANES:(hd // 2 + 1) * V7X_LANES]
    if hd % 2:
        slab = pltpu.roll(slab, HEAD_DIM, 1)
    lane = lax.broadcasted_iota(jnp.int32, slab.shape, 1)
    return jnp.where(lane < HEAD_DIM, slab, (lane == HEAD_DIM).astype(F32)).astype(BF16)


def _params(*sem):
    return pltpu.CompilerParams(dimension_semantics=sem, vmem_limit_bytes=VMEM_LIMIT)


def _full(shape):
    n = len(shape)
    return pl.BlockSpec(shape, lambda *_: (0,) * n)


def _fox_proj_kernel(x_ref, g_ref, wq_ref, wk_ref, wv_ref, wz_ref, *rest, prompt, tiles_per_seq):
    if prompt:
        (wft_ref, bft_ref, u_ref, q_ref, k_ref, v_ref, z_ref, lft_ref,
         kh_ref, vh_ref, nb_ref, carry_ref) = rest
    else:
        wf_ref, bf_ref, q_ref, k_ref, v_ref, z_ref, lf_ref = rest
    h = _rms(x_ref[...], g_ref[...]).astype(BF16)
    q_ref[...] = (_dot(h, wq_ref[...]) * (LOG2E * HEAD_DIM ** -0.5)).astype(BF16)
    k = _dot(h, wk_ref[...])
    v = _dot(h, wv_ref[...])
    k_ref[...] = k
    v_ref[...] = v
    z_ref[...] = _dot(h, wz_ref[...]).astype(BF16)
    if prompt:
        for hh in range(FOX_KV_HEADS):
            kh_ref[0, hh] = k[:, hh * HEAD_DIM:(hh + 1) * HEAD_DIM].astype(BF16)
            vh_ref[0, hh] = _values_with_ones(v, hh)

        @pl.when(pl.program_id(0) % tiles_per_seq == 0)
        def _():
            carry_ref[...] = jnp.zeros_like(carry_ref)

        lft = _log_sigmoid(_dot_nt(wft_ref[...], h) + bft_ref[...])
        lft_ref[0] = lft
        cum, carry = _cumsum_lanes(lft, carry_ref[:, 0:1], u_ref[...])
        nb_ref[0] = -LOG2E * cum
        carry_ref[...] = jnp.broadcast_to(carry, carry_ref.shape)
    else:
        f = _dot(h, wf_ref[...])[:, :FOX_HEADS] + bf_ref[...]
        lf_ref[...] = _log_sigmoid(f)


def _fox_proj(x, g, w, b_f, *, batch=None):
    rows, d = x.shape
    prompt = batch is not None
    tm = min(ROW_TILE, rows)
    nt = rows // tm
    wq, wk, wv, wz, wf, wft = w
    kvw = wk.shape[1]
    ins = [x, g.reshape(1, d), wq, wk, wv, wz]
    in_specs = [pl.BlockSpec((tm, d), lambda i: (i, 0)), _full((1, d)), _full(wq.shape), _full(wk.shape),
                _full(wv.shape), _full(wz.shape)]
    out_shape = [jax.ShapeDtypeStruct((rows, wq.shape[1]), BF16), jax.ShapeDtypeStruct((rows, kvw), F32),
                 jax.ShapeDtypeStruct((rows, kvw), F32), jax.ShapeDtypeStruct((rows, wz.shape[1]), BF16)]
    out_specs = [pl.BlockSpec((tm, wq.shape[1]), lambda i: (i, 0)), pl.BlockSpec((tm, kvw), lambda i: (i, 0)),
                 pl.BlockSpec((tm, kvw), lambda i: (i, 0)), pl.BlockSpec((tm, wz.shape[1]), lambda i: (i, 0))]
    scratch = []
    tps = 1
    if prompt:
        seq = rows // batch
        tps = seq // tm
        ins += [wft, b_f.reshape(FOX_HEADS, 1), _upper_tri_host(min(CUM_GROUP, tm))]
        in_specs += [_full(wft.shape), _full((FOX_HEADS, 1)), _full((min(CUM_GROUP, tm),) * 2)]
        per_head = jax.ShapeDtypeStruct((batch, FOX_HEADS, seq), F32)
        out_shape += [per_head, jax.ShapeDtypeStruct((batch, FOX_KV_HEADS, seq, HEAD_DIM), BF16),
                      jax.ShapeDtypeStruct((batch, FOX_KV_HEADS, seq, V7X_LANES), BF16), per_head]
        ph_spec = pl.BlockSpec((1, FOX_HEADS, tm), lambda i: (i // tps, 0, i % tps))
        out_specs += [ph_spec,
                      pl.BlockSpec((1, FOX_KV_HEADS, tm, HEAD_DIM), lambda i: (i // tps, 0, i % tps, 0)),
                      pl.BlockSpec((1, FOX_KV_HEADS, tm, V7X_LANES), lambda i: (i // tps, 0, i % tps, 0)),
                      ph_spec]
        scratch = [pltpu.VMEM((FOX_HEADS, V7X_LANES), F32)]
    else:
        ins += [wf, b_f.reshape(1, FOX_HEADS)]
        in_specs += [_full(wf.shape), _full((1, FOX_HEADS))]
        out_shape.append(jax.ShapeDtypeStruct((rows, FOX_HEADS), F32))
        out_specs.append(pl.BlockSpec((tm, FOX_HEADS), lambda i: (i, 0)))
    return pl.pallas_call(
        functools.partial(_fox_proj_kernel, prompt=prompt, tiles_per_seq=tps),
        grid=(nt,), in_specs=in_specs, out_specs=out_specs, out_shape=out_shape,
        scratch_shapes=scratch, compiler_params=_params("arbitrary"),
        name="fox_proj_prompt" if prompt else "fox_proj_sample",
    )(*ins)


def _upper_tri_host(n):
    return jnp.asarray(np.triu(np.ones((n, n), np.float32)), dtype=BF16)


def _attn_kernel(*refs, heads, shared_kv, dk, blk, has_bias):
    if has_bias:
        q_ref, k_ref, v_ref, nb_ref, o_ref = refs
    else:
        q_ref, k_ref, v_ref, o_ref = refs
    gi = pl.program_id(1)
    qi = pl.program_id(2)
    rc = min(ATTN_ROW_CHUNK, blk)

    def key_block(j, states):
        diagonal = states is None
        out = []
        off = pl.multiple_of(j * blk, blk)
        if has_bias:
            tile = nb_ref[0, pl.ds(pl.multiple_of(gi * heads // V7X_SUBLANES * V7X_SUBLANES, V7X_SUBLANES),
                                   V7X_SUBLANES), pl.ds(off, blk)]
            nb = jnp.where(gi % (V7X_SUBLANES // heads) == 0, tile[:heads], tile[heads:])
        for g in range(heads):
            kv = 0 if shared_kv else g
            for r0 in range(0, blk, rc):
                kw = r0 + rc if diagonal else blk
                s = _dot_nt(q_ref[r0:r0 + rc, g * dk:(g + 1) * dk], k_ref[0, kv, pl.ds(off, kw), :])
                if has_bias:
                    s = s + nb[g:g + 1, :kw]
                vj = v_ref[0, kv, pl.ds(off, kw), :]
                if diagonal:
                    row = r0 + lax.broadcasted_iota(jnp.int32, s.shape, 0)
                    col = lax.broadcasted_iota(jnp.int32, s.shape, 1)
                    s = jnp.where(row >= col, s, -jnp.inf)
                    m_new = jnp.max(s, axis=1, keepdims=True)
                    acc_new = _dot(jnp.exp2(s - m_new).astype(BF16), vj)
                else:
                    m_old, acc_old = states[len(out)]
                    m_new = jnp.maximum(m_old, jnp.max(s, axis=1, keepdims=True))
                    acc_new = (jnp.exp2(m_old - m_new) * acc_old
                               + _dot(jnp.exp2(s - m_new).astype(BF16), vj))
                out.append((m_new, acc_new))
        return tuple(out)

    states = lax.fori_loop(0, qi, key_block, key_block(qi, None))
    for i, (_, acc) in enumerate(states):
        g, r0 = divmod(i * rc, blk)
        o_ref[r0:r0 + rc, g * HEAD_DIM:(g + 1) * HEAD_DIM] = (
            acc[:, :HEAD_DIM] / acc[:, HEAD_DIM:HEAD_DIM + 1]).astype(BF16)


def _causal_attention(q, kh, vh, nb, *, heads, shared_kv, dk):
    batch, nkv, seq, _ = kh.shape
    kvb = 1 if shared_kv else heads
    groups = nkv // kvb
    blk = min(ATTN_BLOCK, seq)
    nq = seq // blk
    ins = [q, kh, vh]
    in_specs = [pl.BlockSpec((blk, heads * dk), lambda b, g, i: (b * nq + i, g)),
                pl.BlockSpec((1, kvb, seq, dk), lambda b, g, i: (b, g, 0, 0)),
                pl.BlockSpec((1, kvb, seq, V7X_LANES), lambda b, g, i: (b, g, 0, 0))]
    if nb is not None:
        assert shared_kv and 2 * heads == V7X_SUBLANES
        ins.append(nb)
        in_specs.append(pl.BlockSpec((1, nb.shape[1], seq), lambda b, g, i: (b, 0, 0)))
    return pl.pallas_call(
        functools.partial(_attn_kernel, heads=heads, shared_kv=shared_kv, dk=dk, blk=blk,
                          has_bias=nb is not None),
        grid=(batch, groups, nq), in_specs=in_specs,
        out_specs=pl.BlockSpec((blk, heads * HEAD_DIM), lambda b, g, i: (b * nq + i, g)),
        out_shape=jax.ShapeDtypeStruct((batch * seq, groups * heads * HEAD_DIM), BF16),
        compiler_params=_params("parallel", "parallel", "arbitrary"),
        name="causal_attention_bias" if nb is not None else "causal_attention",
    )(*ins)


def _out_kernel(o_ref, z_ref, x_ref, w_ref, g_ref, y_ref):
    a = (o_ref[...].astype(F32) * _silu(z_ref[...].astype(F32))).astype(BF16)
    y_ref[...] = x_ref[...] + _rms(_dot(a, w_ref[...]), g_ref[...])


def _gated_out(o, z, x, w, g):
    rows, d = x.shape
    tm = min(ROW_TILE, rows)
    width = o.shape[1]
    return pl.pallas_call(
        _out_kernel, grid=(rows // tm,),
        in_specs=[pl.BlockSpec((tm, width), lambda i: (i, 0)), pl.BlockSpec((tm, width), lambda i: (i, 0)),
                  pl.BlockSpec((tm, d), lambda i: (i, 0)), _full(w.shape), _full((1, d))],
        out_specs=pl.BlockSpec((tm, d), lambda i: (i, 0)),
        out_shape=jax.ShapeDtypeStruct((rows, d), F32),
        compiler_params=_params("parallel"), name="gated_out",
    )(o, z, x, w, g.reshape(1, d))


def _decode_kernel(pt_ref, *refs, key_pos_minor, new_pos_minor, has_bias, pages, page, n_chunks):
    it = iter(refs)
    n_parts = len(key_pos_minor)
    n_pools = n_parts + (2 if has_bias else 0)
    q_refs = [next(it) for _ in range(n_parts)]
    knew_refs = [next(it) for _ in range(n_parts)]
    vnew_ref = next(it) if has_bias else None
    lfnew_ref = next(it) if has_bias else None
    pool_refs = [next(it) for _ in range(n_pools)]
    o_ref = next(it)
    bufs = [next(it) for _ in range(n_pools)]
    sem = next(it)
    b = pl.program_id(0)

    def copies(seq, chunk, slot):
        return [pltpu.make_async_copy(pool.at[pt_ref[seq, chunk * pages + j]], buf.at[slot, j],
                                      sem.at[slot, i])
                for i, (pool, buf) in enumerate(zip(pool_refs, bufs)) for j in range(pages)]

    @pl.when(b == 0)
    def _():
        for cp in copies(0, 0, 0):
            cp.start()

    qs = [r[0] for r in q_refs]

    def scores(keys, pos_minor):
        return sum((_dot if pm else _dot_nt)(q, k) for q, k, pm in zip(qs, keys, pos_minor))

    def update(state, s, values, values_pos_minor):
        m, l, acc = state
        w = s.shape[1] // len(values)
        m_new = jnp.maximum(m, jnp.max(s, axis=1, keepdims=True))
        alpha = jnp.exp2(m - m_new)
        p = jnp.exp2(s - m_new)
        l = alpha * l + jnp.sum(p, axis=1, keepdims=True)
        pb = p.astype(BF16)
        pv = sum((_dot_nt if values_pos_minor else _dot)(pb[:, j * w:(j + 1) * w], vj)
                 for j, vj in enumerate(values))
        return m_new, l, alpha * acc + pv

    def grouped(buf, slot, pos_minor):
        return [jnp.concatenate([buf[slot, j + i] for i in range(DEC_PAGE_GROUP)],
                                axis=1 if pos_minor else 0).astype(BF16)
                for j in range(0, pages, DEC_PAGE_GROUP)]

    def tile_rows(nb):
        return jnp.concatenate([nb] * (DEC_ROWS // FOX_HEADS), axis=0)

    dv = bufs[n_parts].shape[2] if has_bias else bufs[0].shape[3]
    state = (jnp.full((DEC_ROWS, 1), -jnp.inf, F32), jnp.zeros((DEC_ROWS, 1), F32),
             jnp.zeros((DEC_ROWS, dv), F32))
    carry = jnp.zeros((FOX_HEADS, 1), F32)
    for c in range(n_chunks):
        slot = c % 2
        if c + 1 < n_chunks:
            for cp in copies(b, c + 1, 1 - slot):
                cp.start()
        else:
            @pl.when(b + 1 < pl.num_programs(0))
            def _():
                for cp in copies(b + 1, 0, 1 - slot):
                    cp.start()
        for cp in copies(b, c, slot):
            cp.wait()
        kb = [grouped(bufs[i], slot, key_pos_minor[i]) for i in range(n_parts)]
        s = jnp.concatenate([scores([kb[i][j] for i in range(n_parts)], key_pos_minor)
                             for j in range(pages // DEC_PAGE_GROUP)], axis=1)
        if has_bias:
            lf = jnp.concatenate([bufs[n_parts + 1][slot, j] for j in range(pages)], axis=1)
            cum, carry = _cumsum_lanes(lf, carry)
            s = s - tile_rows(LOG2E * cum)
            values = grouped(bufs[n_parts], slot, True)
        else:
            values = kb[0]
        state = update(state, s, values, has_bias)

    def pad(a):
        return jnp.concatenate([a, jnp.zeros((page - a.shape[0], a.shape[1]), a.dtype)], axis=0)

    knew = [(r[0] if pm else pad(r[0])).astype(BF16) for r, pm in zip(knew_refs, new_pos_minor)]
    sn = scores(knew, new_pos_minor)
    if has_bias:
        cum_new, _ = _cumsum_lanes(lfnew_ref[0], carry)
        sn = sn - tile_rows(LOG2E * cum_new)
        vnew = pad(vnew_ref[0]).astype(BF16)
    else:
        vnew = knew[0]
    tok = lax.broadcasted_iota(jnp.int32, sn.shape, 0) // FOX_HEADS
    new = lax.broadcasted_iota(jnp.int32, sn.shape, 1)
    sn = jnp.where(new <= tok, sn, -jnp.inf)
    _, l, acc = update(state, sn, [vnew], False)
    o = acc / l
    if has_bias:
        kvh = (lax.broadcasted_iota(jnp.int32, (DEC_ROWS, HEAD_DIM), 0) % FOX_HEADS) // FOX_GROUP
        o = sum(jnp.where(kvh == k, o[:, k * HEAD_DIM:(k + 1) * HEAD_DIM], 0.0)
                for k in range(FOX_KV_HEADS))
    o_ref[0] = o


def _decode_attention(page_table, qs, news, pools, key_pos_minor, new_pos_minor, *,
                      vnew=None, lfnew=None, v_pool=None, lf_pool=None):
    bd, n_pages = page_table.shape
    has_bias = lf_pool is not None
    pages = DEC_PAGES if n_pages % DEC_PAGES == 0 else 4
    page = pools[0].shape[2 if key_pos_minor[0] else 1]
    dv = v_pool.shape[1] if has_bias else pools[0].shape[2]

    n_chunks = n_pages // pages
    assert n_chunks % 2 == 0

    def seq_spec(a):
        return pl.BlockSpec((1,) + a.shape[1:], lambda b, pt: (b, 0, 0))

    ins = list(qs) + list(news)
    if has_bias:
        ins += [vnew, lfnew]
    in_specs = [seq_spec(a) for a in ins]
    all_pools = list(pools) + ([v_pool, lf_pool] if has_bias else [])
    ins += all_pools
    in_specs += [pl.BlockSpec(memory_space=pl.ANY)] * len(all_pools)
    scratch = [pltpu.VMEM((2, pages) + pool.shape[1:], F32) for pool in all_pools]
    scratch.append(pltpu.SemaphoreType.DMA((2, len(all_pools))))
    out_w = HEAD_DIM if has_bias else dv
    return pl.pallas_call(
        functools.partial(_decode_kernel, key_pos_minor=tuple(key_pos_minor),
                          new_pos_minor=tuple(new_pos_minor), has_bias=has_bias, pages=pages, page=page,
                          n_chunks=n_chunks),
        grid_spec=pltpu.PrefetchScalarGridSpec(
            num_scalar_prefetch=1, grid=(bd,), in_specs=in_specs,
            out_specs=pl.BlockSpec((1, DEC_ROWS, out_w), lambda b, pt: (b, 0, 0)),
            scratch_shapes=scratch),
        out_shape=jax.ShapeDtypeStruct((bd, DEC_ROWS, out_w), F32),
        compiler_params=_params("arbitrary"),
        name="decode_fox" if has_bias else "decode_mla",
    )(page_table, *ins)


def _pad_axis(a, axis, size):
    pad = [(0, 0)] * a.ndim
    pad[axis] = (0, size - a.shape[axis])
    return jnp.pad(a, pad)


def _conv_proj_kernel(x_ref, g_ref, wa_ref, wb_ref, wz_ref, u_ref, z_ref):
    h = _rms(x_ref[...], g_ref[...]).astype(BF16)
    u_ref[...] = _dot(h, wa_ref[...]) * jax.nn.sigmoid(_dot(h, wb_ref[...]))
    z_ref[...] = _dot(h, wz_ref[...]).astype(BF16)


def _conv_proj(x, g, w):
    rows, d = x.shape
    tm = min(ROW_TILE, rows)
    wa, wb, wz = w
    ch = wa.shape[1]
    return pl.pallas_call(
        _conv_proj_kernel, grid=(rows // tm,),
        in_specs=[pl.BlockSpec((tm, d), lambda i: (i, 0)), _full((1, d)), _full(wa.shape), _full(wb.shape),
                  _full(wz.shape)],
        out_specs=[pl.BlockSpec((tm, ch), lambda i: (i, 0)), pl.BlockSpec((tm, ch), lambda i: (i, 0))],
        out_shape=[jax.ShapeDtypeStruct((rows, ch), F32), jax.ShapeDtypeStruct((rows, ch), BF16)],
        compiler_params=_params("parallel"), name="conv_proj",
    )(x, g.reshape(1, d), wa, wb, wz)


def _ln_silu(c, g, b):
    cc = c - jnp.mean(c, axis=-1, keepdims=True)
    var = jnp.mean(cc * cc, axis=-1, keepdims=True)
    return _silu(cc * lax.rsqrt(var + NORM_EPS) * g + b)


def _conv_prompt_kernel(u_ref, halo_ref, dw_ref, dwb_ref, lng_ref, lnb_ref, c_ref, ext_ref, sh_ref, *, tm):
    first = pl.program_id(1) == 0
    ext_ref[0:CONV_HALO, :] = jnp.where(first, 0.0, halo_ref[0])
    ext_ref[CONV_HALO:, :] = u_ref[0]
    n_sh = tm + CONV_HALO - V7X_SUBLANES
    for sh in range(1, V7X_SUBLANES):
        sh_ref[sh - 1] = ext_ref[sh:sh + n_sh, :]
    base = CONV_HALO - (CONV_TAPS - 1)
    for r0 in range(0, tm, CONV_CHUNK):
        acc = jnp.zeros((CONV_CHUNK, u_ref.shape[2]), F32)
        for j in range(CONV_TAPS):
            tiles, sh = divmod(base + j, V7X_SUBLANES)
            src = ext_ref if sh == 0 else sh_ref.at[sh - 1]
            row = r0 + tiles * V7X_SUBLANES
            acc = acc + dw_ref[j:j + 1, :] * src[row:row + CONV_CHUNK, :]
        c = _ln_silu(acc + dwb_ref[...], lng_ref[...], lnb_ref[...])
        c_ref[r0:r0 + CONV_CHUNK, :] = c.astype(BF16)


def _conv_prompt(u, dw, dw_b, ln_g, ln_b):
    batch, seq, ch = u.shape
    tm = min(CONV_ROW_TILE, seq)
    nt = seq // tm
    hpt = tm // CONV_HALO
    return pl.pallas_call(
        functools.partial(_conv_prompt_kernel, tm=tm), grid=(batch, nt),
        in_specs=[pl.BlockSpec((1, tm, ch), lambda b, i: (b, i, 0)),
                  pl.BlockSpec((1, CONV_HALO, ch), lambda b, i: (b, jnp.maximum(i * hpt - 1, 0), 0)),
                  _full(dw.shape), _full((1, ch)), _full((1, ch)), _full((1, ch))],
        out_specs=pl.BlockSpec((tm, ch), lambda b, i: (b * nt + i, 0)),
        out_shape=jax.ShapeDtypeStruct((batch * seq, ch), BF16),
        scratch_shapes=[pltpu.VMEM((tm + CONV_HALO, ch), F32),
                        pltpu.VMEM((V7X_SUBLANES - 1, tm + CONV_HALO - V7X_SUBLANES, ch), F32)],
        compiler_params=_params("parallel", "arbitrary"), name="conv_prompt",
    )(u, u, dw, dw_b.reshape(1, ch), ln_g.reshape(1, ch), ln_b.reshape(1, ch))


def _conv_sample_kernel(st_ref, u_ref, dw_ref, dwb_ref, lng_ref, lnb_ref, c_ref, so_ref):
    n_state, t_new = st_ref.shape[0], u_ref.shape[0]

    def ext(i):
        return st_ref[i] if i < n_state else u_ref[i - n_state]

    for i in range(n_state):
        so_ref[i] = ext(i + t_new)
    for t in range(t_new):
        acc = dw_ref[0:1, :] * ext(t)
        for j in range(1, CONV_TAPS):
            acc = acc + dw_ref[j:j + 1, :] * ext(t + j)
        c_ref[t] = _ln_silu(acc + dwb_ref[...], lng_ref[...], lnb_ref[...])


def _conv_sample(state, u_new, dw, dw_b, ln_g, ln_b):
    n_state, bd, ch = state.shape
    t_new = u_new.shape[0]
    sb = min(CONV_SEQ_BLOCK, bd)
    return pl.pallas_call(
        _conv_sample_kernel, grid=(bd // sb,),
        in_specs=[pl.BlockSpec((n_state, sb, ch), lambda i: (0, i, 0)),
                  pl.BlockSpec((t_new, sb, ch), lambda i: (0, i, 0)),
                  _full(dw.shape), _full((1, ch)), _full((1, ch)), _full((1, ch))],
        out_specs=[pl.BlockSpec((t_new, sb, ch), lambda i: (0, i, 0)),
                   pl.BlockSpec((n_state, sb, ch), lambda i: (0, i, 0))],
        out_shape=[jax.ShapeDtypeStruct((t_new, bd, ch), F32), jax.ShapeDtypeStruct((n_state, bd, ch), F32)],
        compiler_params=_params("parallel"), name="conv_sample",
    )(state, u_new, dw, dw_b.reshape(1, ch), ln_g.reshape(1, ch), ln_b.reshape(1, ch))


def _rope_lane_group(x, c, sa, sb):
    return x * c + pltpu.roll(x, V7X_LANES - MLA_ROPE // 2, 1) * sa + pltpu.roll(x, MLA_ROPE // 2, 1) * sb


def _mla_proj_kernel(x_ref, g_ref, wcq_ref, wckv_ref, wkpe_ref, wz_ref, qn_ref, kvn_ref, wuq_ref,
                     c_ref, sa_ref, sb_ref, *rest, prompt):
    if prompt:
        wukn_ref, wuv_ref, q_ref, ckv_ref, kpe_ref, z_ref, kh_ref, vh_ref = rest
    else:
        q_ref, ckv_ref, kpe_ref, z_ref = rest
    scale = LOG2E * (HEAD_DIM + MLA_ROPE) ** -0.5
    c, sa, sb = c_ref[...], sa_ref[...], sb_ref[...]
    h = _rms(x_ref[...], g_ref[...]).astype(BF16)
    cq = _rms(_dot(h, wcq_ref[...]), qn_ref[...]).astype(BF16)
    ckv = _rms(_dot(h, wckv_ref[...]), kvn_ref[...])
    ckv_ref[...] = ckv
    kp = _rope_lane_group(_dot(h, wkpe_ref[...]), c, sa, sb)
    kpe_ref[...] = kp[:, HEAD_DIM:HEAD_DIM + MLA_ROPE]
    z_ref[...] = _dot(h, wz_ref[...]).astype(BF16)
    q = _dot(cq, wuq_ref[...])
    for hd in range(MLA_HEADS):
        sl = slice(hd * MLA_QK_PAD, (hd + 1) * MLA_QK_PAD)
        q_ref[:, sl] = (_rope_lane_group(q[:, sl], c, sa, sb) * scale).astype(BF16)
    if prompt:
        ckv_b = ckv.astype(BF16)
        kn = _dot(ckv_b, wukn_ref[...])
        v = _dot(ckv_b, wuv_ref[...])
        for hd in range(MLA_HEADS):
            kh_ref[0, hd] = (kn[:, hd * MLA_QK_PAD:(hd + 1) * MLA_QK_PAD] + kp).astype(BF16)
            vh_ref[0, hd] = _values_with_ones(v, hd)


def _mla_proj(x, g, w, q_norm, kv_norm, tables, *, batch=None):
    rows, d = x.shape
    prompt = batch is not None
    tm = min(ROW_TILE // 2, rows)
    nt = rows // tm
    wcq, wckv, wkpe, wz, wuq, wukn, wuv = w
    tab_rows = tables[0].shape[0]
    tpt = tab_rows // tm
    ins = [x, g.reshape(1, d), wcq, wckv, wkpe, wz, q_norm.reshape(1, -1), kv_norm.reshape(1, -1), wuq,
           *tables]
    in_specs = [pl.BlockSpec((tm, d), lambda i: (i, 0)), _full((1, d)), _full(wcq.shape), _full(wckv.shape),
                _full(wkpe.shape), _full(wz.shape), _full((1, wcq.shape[1])), _full((1, wckv.shape[1])),
                _full(wuq.shape)]
    in_specs += [pl.BlockSpec((tm, V7X_LANES), lambda i: (i % tpt, 0)) for _ in tables]
    qw = wuq.shape[1]
    out_shape = [jax.ShapeDtypeStruct((rows, qw), BF16), jax.ShapeDtypeStruct((rows, MLA_KV_LORA), F32),
                 jax.ShapeDtypeStruct((rows, MLA_ROPE), F32), jax.ShapeDtypeStruct((rows, wz.shape[1]), BF16)]
    out_specs = [pl.BlockSpec((tm, qw), lambda i: (i, 0)), pl.BlockSpec((tm, MLA_KV_LORA), lambda i: (i, 0)),
                 pl.BlockSpec((tm, MLA_ROPE), lambda i: (i, 0)), pl.BlockSpec((tm, wz.shape[1]), lambda i: (i, 0))]
    if prompt:
        seq = rows // batch
        tps = seq // tm
        ins += [wukn, wuv]
        in_specs += [_full(wukn.shape), _full(wuv.shape)]
        out_shape += [jax.ShapeDtypeStruct((batch, MLA_HEADS, seq, MLA_QK_PAD), BF16),
                      jax.ShapeDtypeStruct((batch, MLA_HEADS, seq, V7X_LANES), BF16)]
        out_specs += [pl.BlockSpec((1, MLA_HEADS, tm, MLA_QK_PAD), lambda i: (i // tps, 0, i % tps, 0)),
                      pl.BlockSpec((1, MLA_HEADS, tm, V7X_LANES), lambda i: (i // tps, 0, i % tps, 0))]
    return pl.pallas_call(
        functools.partial(_mla_proj_kernel, prompt=prompt), grid=(nt,), in_specs=in_specs,
        out_specs=out_specs, out_shape=out_shape, compiler_params=_params("parallel"),
        name="mla_proj_prompt" if prompt else "mla_proj_sample",
    )(*ins)


def _rope_tables(positions):
    half = MLA_ROPE // 2
    inv = ROPE_THETA ** (-np.arange(half, dtype=np.float32) / half)
    ang = np.asarray(positions, np.float32)[:, None] * inv[None, :]
    cos, sin = np.cos(ang).astype(np.float32), np.sin(ang).astype(np.float32)
    n = ang.shape[0]
    c = np.zeros((n, V7X_LANES), np.float32)
    sa = np.zeros((n, V7X_LANES), np.float32)
    sb = np.zeros((n, V7X_LANES), np.float32)
    c[:, :HEAD_DIM] = 1.0
    c[:, HEAD_DIM:HEAD_DIM + half] = cos
    c[:, HEAD_DIM + half:HEAD_DIM + 2 * half] = cos
    sa[:, HEAD_DIM:HEAD_DIM + half] = -sin
    sb[:, HEAD_DIM + half:HEAD_DIM + 2 * half] = sin
    return jnp.asarray(c), jnp.asarray(sa), jnp.asarray(sb)


def _headwise_kernel(x_ref, w_ref, o_ref):
    o_ref[0] = _dot(x_ref[0], w_ref[0]).astype(o_ref.dtype)


def _headwise_matmul(x, w, dtype):
    nh, rows, k = x.shape
    n = w.shape[2]
    return pl.pallas_call(
        _headwise_kernel, grid=(nh,),
        in_specs=[pl.BlockSpec((1, rows, k), lambda i: (i, 0, 0)), pl.BlockSpec((1, k, n), lambda i: (i, 0, 0))],
        out_specs=pl.BlockSpec((1, rows, n), lambda i: (i, 0, 0)),
        out_shape=jax.ShapeDtypeStruct((nh, rows, n), dtype),
        compiler_params=_params("parallel"), name="headwise_matmul",
    )(x, w)


def _fox_weights(w_in):
    d = w_in.shape[0]
    width = FOX_HEADS * HEAD_DIM
    kvw = FOX_KV_HEADS * HEAD_DIM
    o1, o2, o3 = width + kvw, width + 2 * kvw, 2 * width + 2 * kvw
    wb = w_in.astype(BF16)
    wf = jnp.zeros((d, V7X_LANES), BF16).at[:, :FOX_HEADS].set(wb[:, o3:])
    return wb[:, :width], wb[:, width:o1], wb[:, o1:o2], wb[:, o2:o3], wf, wb[:, o3:].T


def _fox_layer(xp, xs, cache_k, cache_v, cache_logf, page_table, g_pre, w_in, b_f, w_out, g_post, dims):
    batch, seq, bd, t_new = dims
    w = _fox_weights(w_in)
    w_out = w_out.astype(BF16)
    n_pool, page = cache_k.shape[:2]
    kvw = FOX_KV_HEADS * HEAD_DIM

    q, k, v, z, lft, kh, vh, nb = _fox_proj(xp, g_pre, w, b_f, batch=batch)
    o = _causal_attention(q, kh, vh, nb, heads=FOX_GROUP, shared_kv=True, dk=HEAD_DIM)
    xp = _gated_out(o, z, xp, w_out, g_post)

    qs, ks, vs, zs, lfs = _fox_proj(xs, g_pre, w, b_f)
    own = (np.arange(FOX_HEADS)[:, None] // FOX_GROUP == np.arange(FOX_KV_HEADS)[None, :])
    qbd = (qs.reshape(bd, t_new, FOX_HEADS, 1, HEAD_DIM)
           * jnp.asarray(own, BF16)[None, None, :, :, None]).reshape(bd, t_new * FOX_HEADS, kvw)
    k_pool = cache_k.transpose(0, 2, 3, 1).reshape(n_pool, kvw, page)
    v_pool = cache_v.transpose(0, 2, 3, 1).reshape(n_pool, kvw, page)
    lf_pool = cache_logf.transpose(0, 2, 1)
    os_ = _decode_attention(
        page_table, [qbd], [_pad_axis(ks.reshape(bd, t_new, kvw), 1, V7X_SUBLANES)], [k_pool], (True,), (False,),
        vnew=_pad_axis(vs.reshape(bd, t_new, kvw), 1, V7X_SUBLANES),
        lfnew=_pad_axis(lfs.reshape(bd, t_new, FOX_HEADS).transpose(0, 2, 1), 2, page),
        v_pool=v_pool, lf_pool=lf_pool)
    xs = _gated_out(os_.reshape(bd * t_new, FOX_HEADS * HEAD_DIM), zs, xs, w_out, g_post)

    new_p = (k.reshape(batch, seq, FOX_KV_HEADS, HEAD_DIM), v.reshape(batch, seq, FOX_KV_HEADS, HEAD_DIM),
             lft.transpose(0, 2, 1))
    new_s = (ks.reshape(bd, t_new, FOX_KV_HEADS, HEAD_DIM), vs.reshape(bd, t_new, FOX_KV_HEADS, HEAD_DIM),
             lfs.reshape(bd, t_new, FOX_HEADS))
    return xp, xs, new_p, new_s


def _conv_layer(xp, xs, state, g_pre, w_in, dw, dw_b, ln_g, ln_b, w_out, g_post, dims):
    batch, seq, bd, t_new = dims
    ch = dw.shape[1]
    wb = w_in.astype(BF16)
    w = (wb[:, :ch], wb[:, ch:2 * ch], wb[:, 2 * ch:])
    w_out = w_out.astype(BF16)

    u, z = _conv_proj(xp, g_pre, w)
    u3 = u.reshape(batch, seq, ch)
    c = _conv_prompt(u3, dw, dw_b, ln_g, ln_b)
    xp = _gated_out(c, z, xp, w_out, g_post)

    us, zs = _conv_proj(xs, g_pre, w)
    cs, state_new = _conv_sample(state.transpose(1, 0, 2), us.reshape(bd, t_new, ch).transpose(1, 0, 2),
                                 dw, dw_b, ln_g, ln_b)
    xs = _gated_out(cs.transpose(1, 0, 2).reshape(bd * t_new, ch), zs, xs, w_out, g_post)
    return xp, xs, (u3[:, seq - (CONV_TAPS - 1):],), (state_new.transpose(1, 0, 2),)


def _mla_layer(xp, xs, cache_ckv, cache_kpe, page_table, g_pre, w_in, q_norm, w_uq, kv_norm, w_ukv, w_out,
               g_post, dims):
    batch, seq, bd, t_new = dims
    d = w_in.shape[0]
    page = cache_ckv.shape[1]
    past = page_table.shape[1] * page
    o0, o1, o2 = MLA_Q_LORA, MLA_Q_LORA + MLA_KV_LORA, MLA_Q_LORA + MLA_KV_LORA + MLA_ROPE
    wb = w_in.astype(BF16)
    wkpe = jnp.zeros((d, V7X_LANES), BF16).at[:, HEAD_DIM:HEAD_DIM + MLA_ROPE].set(wb[:, o1:o2])
    dqk = HEAD_DIM + MLA_ROPE
    wuq = jnp.pad(w_uq.astype(BF16).reshape(MLA_Q_LORA, MLA_HEADS, dqk),
                  ((0, 0), (0, 0), (0, MLA_QK_PAD - dqk))).reshape(MLA_Q_LORA, MLA_HEADS * MLA_QK_PAD)
    wukv = w_ukv.astype(BF16).reshape(MLA_KV_LORA, MLA_HEADS, 2 * HEAD_DIM)
    wuk, wuv = wukv[..., :HEAD_DIM], wukv[..., HEAD_DIM:]
    wukn = jnp.pad(wuk, ((0, 0), (0, 0), (0, MLA_QK_PAD - HEAD_DIM))).reshape(MLA_KV_LORA, MLA_HEADS * MLA_QK_PAD)
    w = (wb[:, :o0], wb[:, o0:o1], wkpe, wb[:, o2:], wuq, wukn, wuv.reshape(MLA_KV_LORA, MLA_HEADS * HEAD_DIM))
    w_out = w_out.astype(BF16)

    q, ckv, kpe, z, kh, vh = _mla_proj(xp, g_pre, w, q_norm, kv_norm, _rope_tables(np.arange(seq)), batch=batch)
    o = _causal_attention(q, kh, vh, None, heads=2, shared_kv=False, dk=MLA_QK_PAD)
    xp = _gated_out(o, z, xp, w_out, g_post)

    rows = bd * t_new
    pos_s = np.tile(past + np.arange(t_new), bd)
    qs, ckvs, kpes, zs = _mla_proj(xs, g_pre, w, q_norm, kv_norm, _rope_tables(pos_s))
    q4 = qs.reshape(rows, MLA_HEADS, MLA_QK_PAD)
    q_lat = _headwise_matmul(q4[..., :HEAD_DIM].transpose(1, 0, 2), wuk.transpose(1, 2, 0), BF16)
    q_lat = q_lat.transpose(1, 0, 2).reshape(bd, t_new * MLA_HEADS, MLA_KV_LORA)
    q_pe = q4[..., HEAD_DIM:HEAD_DIM + MLA_ROPE].reshape(bd, t_new * MLA_HEADS, MLA_ROPE)
    o_lat = _decode_attention(
        page_table, [q_lat, q_pe],
        [_pad_axis(ckvs.reshape(bd, t_new, MLA_KV_LORA), 1, V7X_SUBLANES),
         _pad_axis(kpes.reshape(bd, t_new, MLA_ROPE).transpose(0, 2, 1), 2, page)],
        [cache_ckv, cache_kpe.transpose(0, 2, 1)], (False, True), (False, True))
    o_lat = o_lat.astype(BF16).reshape(rows, MLA_HEADS, MLA_KV_LORA).transpose(1, 0, 2)
    os_ = _headwise_matmul(o_lat, wuv.transpose(1, 0, 2), BF16).transpose(1, 0, 2)
    xs = _gated_out(os_.reshape(rows, MLA_HEADS * HEAD_DIM), zs, xs, w_out, g_post)

    new_p = (ckv.reshape(batch, seq, MLA_KV_LORA), kpe.reshape(batch, seq, MLA_ROPE))
    new_s = (ckvs.reshape(bd, t_new, MLA_KV_LORA), kpes.reshape(bd, t_new, MLA_ROPE))
    return xp, xs, new_p, new_s


def kernel(x_prompt, x_sample, cache_k_l0, cache_v_l0, cache_logf_l0, state_conv_l1, cache_ckv_l2, cache_kpe_l2, cache_k_l3, cache_v_l3, cache_logf_l3, page_table, norm_pre_l0, w_in_l0, b_f_l0, w_out_l0, norm_post_l0, norm_pre_l1, w_in_l1, dw_l1, dw_b_l1, ln_g_l1, ln_b_l1, w_out_l1, norm_post_l1, norm_pre_l2, w_in_l2, q_norm_l2, w_uq_l2, kv_norm_l2, w_ukv_l2, w_out_l2, norm_post_l2, norm_pre_l3, w_in_l3, b_f_l3, w_out_l3, norm_post_l3):
    batch, seq, d = x_prompt.shape
    bd, t_new, _ = x_sample.shape
    assert t_new * FOX_HEADS == DEC_ROWS and t_new * MLA_HEADS == DEC_ROWS
    dims = (batch, seq, bd, t_new)
    xp = x_prompt.reshape(batch * seq, d)
    xs = x_sample.reshape(bd * t_new, d)

    xp, xs, (k_p0, v_p0, lf_p0), (k_s0, v_s0, lf_s0) = _fox_layer(
        xp, xs, cache_k_l0, cache_v_l0, cache_logf_l0, page_table,
        norm_pre_l0, w_in_l0, b_f_l0, w_out_l0, norm_post_l0, dims)
    xp, xs, (conv_p1,), (conv_s1,) = _conv_layer(
        xp, xs, state_conv_l1, norm_pre_l1, w_in_l1, dw_l1, dw_b_l1, ln_g_l1, ln_b_l1, w_out_l1,
        norm_post_l1, dims)
    xp, xs, (ckv_p2, kpe_p2), (ckv_s2, kpe_s2) = _mla_layer(
        xp, xs, cache_ckv_l2, cache_kpe_l2, page_table,
        norm_pre_l2, w_in_l2, q_norm_l2, w_uq_l2, kv_norm_l2, w_ukv_l2, w_out_l2, norm_post_l2, dims)
    xp, xs, (k_p3, v_p3, lf_p3), (k_s3, v_s3, lf_s3) = _fox_layer(
        xp, xs, cache_k_l3, cache_v_l3, cache_logf_l3, page_table,
        norm_pre_l3, w_in_l3, b_f_l3, w_out_l3, norm_post_l3, dims)

    return (xp.reshape(batch, seq, d), xs.reshape(bd, t_new, d),
            k_p0, v_p0, lf_p0, k_s0, v_s0, lf_s0,
            conv_p1, conv_s1,
            ckv_p2, kpe_p2, ckv_s2, kpe_s2,
            k_p3, v_p3, lf_p3, k_s3, v_s3, lf_s3)
```

```python
import functools

import numpy as np
import jax
import jax.numpy as jnp
from jax import lax
from jax.experimental import pallas as pl
from jax.experimental.pallas import tpu as pltpu

F32 = jnp.float32
BF16 = jnp.bfloat16

NORM_EPS = 1e-6
LOG2E = 1.4426950408889634
ROPE_THETA = 10000.0
CONV_TAPS = 31

FOX_HEADS = 16
FOX_KV_HEADS = 4
FOX_GROUP = FOX_HEADS // FOX_KV_HEADS
HEAD_DIM = 64
MLA_HEADS = 16
MLA_ROPE = 32
MLA_Q_LORA = 768
MLA_KV_LORA = 256
MLA_QK_PAD = 128

V7X_LANES = 128
V7X_SUBLANES = 8
V7X_VMEM_BYTES = 64 * 1024 * 1024
VMEM_LIMIT = V7X_VMEM_BYTES * 7 // 8

ROW_TILE = 512
ATTN_BLOCK = 512
ATTN_ROW_CHUNK = 512
CONV_ROW_TILE = 256
CONV_CHUNK = 32
CONV_HALO = 32
CONV_SEQ_BLOCK = 32
CUM_GROUP = 256
DEC_ROWS = 64
DEC_PAGES = 16
DEC_PAGE_GROUP = 2
DEC_AHEAD = 2


def _dot(a, b):
    return jnp.dot(a, b, preferred_element_type=F32)


def _dot_nt(a, b):
    return lax.dot_general(a, b, (((1,), (1,)), ((), ())), preferred_element_type=F32)


def _rms(x, g):
    return x * lax.rsqrt(jnp.mean(x * x, axis=-1, keepdims=True) + NORM_EPS) * g


def _log_sigmoid(x):
    return jnp.minimum(x, 0.0) - jnp.log1p(jnp.exp(-jnp.abs(x)))


def _silu(x):
    return x * jax.nn.sigmoid(x)


def _split3(x):
    x1 = x.astype(BF16)
    r = x - x1.astype(F32)
    x2 = r.astype(BF16)
    r = r - x2.astype(F32)
    return x1, x2, r.astype(BF16)


def _upper_tri(n):
    r = lax.broadcasted_iota(jnp.int32, (n, n), 0)
    c = lax.broadcasted_iota(jnp.int32, (n, n), 1)
    return (r <= c).astype(BF16)


def _cumsum_lanes(xt, carry, u=None):
    h, n = xt.shape
    g = min(CUM_GROUP, n)
    ng = n // g
    x = xt if ng == 1 else jnp.concatenate([xt[:, i * g:(i + 1) * g] for i in range(ng)], axis=0)
    if u is None:
        u = _upper_tri(g)
    cum = sum(_dot(p, u) for p in _split3(x))
    if ng > 1:
        r = lax.broadcasted_iota(jnp.int32, (ng * h, ng * h), 0)
        c = lax.broadcasted_iota(jnp.int32, (ng * h, ng * h), 1)
        earlier = ((r % h == c % h) & (c // h < r // h)).astype(BF16)
        tot = jnp.broadcast_to(cum[:, g - 1:g], (ng * h, V7X_LANES))
        off = sum(_dot(earlier, p) for p in _split3(tot))
        cum = cum + jnp.concatenate([off] * (g // V7X_LANES), axis=1)
        carry_rows = jnp.concatenate([carry] * ng, axis=0)
    else:
        carry_rows = carry
    cum = cum + carry_rows
    new_carry = cum[(ng - 1) * h:, g - 1:g]
    if ng > 1:
        cum = jnp.concatenate([cum[i * h:(i + 1) * h] for i in range(ng)], axis=1)
    return cum, new_carry


def _values_with_ones(v, hd):
    slab = v[:, (hd // 2) * V7X_LANES:(hd // 2 + 1) * V7X_LANES]
    if hd % 2:
        slab = pltpu.roll(slab, HEAD_DIM, 1)
    lane = lax.broadcasted_iota(jnp.int32, slab.shape, 1)
    return jnp.where(lane < HEAD_DIM, slab, (lane == HEAD_DIM).astype(F32)).astype(BF16)


def _params(*sem):
    return pltpu.CompilerParams(dimension_semantics=sem, vmem_limit_bytes=VMEM_LIMIT)


def _full(shape):
    n = len(shape)
    return pl.BlockSpec(shape, lambda *_: (0,) * n)


def _fox_proj_kernel(x_ref, g_ref, wq_ref, wk_ref, wv_ref, wz_ref, *rest, prompt, tiles_per_seq):
    if prompt:
        (wft_ref, bft_ref, u_ref, q_ref, k_ref, v_ref, z_ref, lft_ref,
         kh_ref, vh_ref, nb_ref, carry_ref) = rest
    else:
        wf_ref, bf_ref, q_ref, k_ref, v_ref, z_ref, lf_ref = rest
    h = _rms(x_ref[...], g_ref[...]).astype(BF16)
    q_ref[...] = (_dot(h, wq_ref[...]) * (LOG2E * HEAD_DIM ** -0.5)).astype(BF16)
    k = _dot(h, wk_ref[...])
    v = _dot(h, wv_ref[...])
    k_ref[...] = k
    v_ref[...] = v
    z_ref[...] = _dot(h, wz_ref[...]).astype(BF16)
    if prompt:
        for hh in range(FOX_KV_HEADS):
            kh_ref[0, hh] = k[:, hh * HEAD_DIM:(hh + 1) * HEAD_DIM].astype(BF16)
            vh_ref[0, hh] = _values_with_ones(v, hh)

        @pl.when(pl.program_id(0) % tiles_per_seq == 0)
        def _():
            carry_ref[...] = jnp.zeros_like(carry_ref)

        lft = _log_sigmoid(_dot_nt(wft_ref[...], h) + bft_ref[...])
        lft_ref[0] = lft
        cum, carry = _cumsum_lanes(lft, carry_ref[:, 0:1], u_ref[...])
        nb_ref[0] = -LOG2E * cum
        carry_ref[...] = jnp.broadcast_to(carry, carry_ref.shape)
    else:
        f = _dot(h, wf_ref[...])[:, :FOX_HEADS] + bf_ref[...]
        lf_ref[...] = _log_sigmoid(f)


def _fox_proj(x, g, w, b_f, *, batch=None):
    rows, d = x.shape
    prompt = batch is not None
    tm = min(ROW_TILE, rows)
    nt = rows // tm
    wq, wk, wv, wz, wf, wft = w
    kvw = wk.shape[1]
    ins = [x, g.reshape(1, d), wq, wk, wv, wz]
    in_specs = [pl.BlockSpec((tm, d), lambda i: (i, 0)), _full((1, d)), _full(wq.shape), _full(wk.shape),
                _full(wv.shape), _full(wz.shape)]
    out_shape = [jax.ShapeDtypeStruct((rows, wq.shape[1]), BF16), jax.ShapeDtypeStruct((rows, kvw), F32),
                 jax.ShapeDtypeStruct((rows, kvw), F32), jax.ShapeDtypeStruct((rows, wz.shape[1]), BF16)]
    out_specs = [pl.BlockSpec((tm, wq.shape[1]), lambda i: (i, 0)), pl.BlockSpec((tm, kvw), lambda i: (i, 0)),
                 pl.BlockSpec((tm, kvw), lambda i: (i, 0)), pl.BlockSpec((tm, wz.shape[1]), lambda i: (i, 0))]
    scratch = []
    tps = 1
    if prompt:
        seq = rows // batch
        tps = seq // tm
        ins += [wft, b_f.reshape(FOX_HEADS, 1), _upper_tri_host(min(CUM_GROUP, tm))]
        in_specs += [_full(wft.shape), _full((FOX_HEADS, 1)), _full((min(CUM_GROUP, tm),) * 2)]
        per_head = jax.ShapeDtypeStruct((batch, FOX_HEADS, seq), F32)
        out_shape += [per_head, jax.ShapeDtypeStruct((batch, FOX_KV_HEADS, seq, HEAD_DIM), BF16),
                      jax.ShapeDtypeStruct((batch, FOX_KV_HEADS, seq, V7X_LANES), BF16), per_head]
        ph_spec = pl.BlockSpec((1, FOX_HEADS, tm), lambda i: (i // tps, 0, i % tps))
        out_specs += [ph_spec,
                      pl.BlockSpec((1, FOX_KV_HEADS, tm, HEAD_DIM), lambda i: (i // tps, 0, i % tps, 0)),
                      pl.BlockSpec((1, FOX_KV_HEADS, tm, V7X_LANES), lambda i: (i // tps, 0, i % tps, 0)),
                      ph_spec]
        scratch = [pltpu.VMEM((FOX_HEADS, V7X_LANES), F32)]
    else:
        ins += [wf, b_f.reshape(1, FOX_HEADS)]
        in_specs += [_full(wf.shape), _full((1, FOX_HEADS))]
        out_shape.append(jax.ShapeDtypeStruct((rows, FOX_HEADS), F32))
        out_specs.append(pl.BlockSpec((tm, FOX_HEADS), lambda i: (i, 0)))
    return pl.pallas_call(
        functools.partial(_fox_proj_kernel, prompt=prompt, tiles_per_seq=tps),
        grid=(nt,), in_specs=in_specs, out_specs=out_specs, out_shape=out_shape,
        scratch_shapes=scratch, compiler_params=_params("arbitrary"),
        name="fox_proj_prompt" if prompt else "fox_proj_sample",
    )(*ins)


def _upper_tri_host(n):
    return jnp.asarray(np.triu(np.ones((n, n), np.float32)), dtype=BF16)


def _attn_kernel(*refs, heads, shared_kv, dk, blk, has_bias):
    if has_bias:
        q_ref, k_ref, v_ref, nb_ref, o_ref = refs
    else:
        q_ref, k_ref, v_ref, o_ref = refs
    gi = pl.program_id(1)
    qi = pl.program_id(2)
    rc = min(ATTN_ROW_CHUNK, blk)

    def key_block(j, states):
        diagonal = states is None
        out = []
        off = pl.multiple_of(j * blk, blk)
        if has_bias:
            tile = nb_ref[0, pl.ds(pl.multiple_of(gi * heads // V7X_SUBLANES * V7X_SUBLANES, V7X_SUBLANES),
                                   V7X_SUBLANES), pl.ds(off, blk)]
            nb = jnp.where(gi % (V7X_SUBLANES // heads) == 0, tile[:heads], tile[heads:])
        for g in range(heads):
            kv = 0 if shared_kv else g
            for r0 in range(0, blk, rc):
                kw = r0 + rc if diagonal else blk
                s = _dot_nt(q_ref[r0:r0 + rc, g * dk:(g + 1) * dk], k_ref[0, kv, pl.ds(off, kw), :])
                if has_bias:
                    s = s + nb[g:g + 1, :kw]
                vj = v_ref[0, kv, pl.ds(off, kw), :]
                if diagonal:
                    row = r0 + lax.broadcasted_iota(jnp.int32, s.shape, 0)
                    col = lax.broadcasted_iota(jnp.int32, s.shape, 1)
                    s = jnp.where(row >= col, s, -jnp.inf)
                    m_new = jnp.max(s, axis=1, keepdims=True)
                    acc_new = _dot(jnp.exp2(s - m_new).astype(BF16), vj)
                else:
                    m_old, acc_old = states[len(out)]
                    m_new = jnp.maximum(m_old, jnp.max(s, axis=1, keepdims=True))
                    acc_new = (jnp.exp2(m_old - m_new) * acc_old
                               + _dot(jnp.exp2(s - m_new).astype(BF16), vj))
                out.append((m_new, acc_new))
        return tuple(out)

    states = lax.fori_loop(0, qi, key_block, key_block(qi, None))
    for i, (_, acc) in enumerate(states):
        g, r0 = divmod(i * rc, blk)
        o_ref[r0:r0 + rc, g * HEAD_DIM:(g + 1) * HEAD_DIM] = (
            acc[:, :HEAD_DIM] / acc[:, HEAD_DIM:HEAD_DIM + 1]).astype(BF16)


def _causal_attention(q, kh, vh, nb, *, heads, shared_kv, dk):
    batch, nkv, seq, _ = kh.shape
    kvb = 1 if shared_kv else heads
    groups = nkv // kvb
    blk = min(ATTN_BLOCK, seq)
    nq = seq // blk
    ins = [q, kh, vh]
    in_specs = [pl.BlockSpec((blk, heads * dk), lambda b, g, i: (b * nq + i, g)),
                pl.BlockSpec((1, kvb, seq, dk), lambda b, g, i: (b, g, 0, 0)),
                pl.BlockSpec((1, kvb, seq, V7X_LANES), lambda b, g, i: (b, g, 0, 0))]
    if nb is not None:
        assert shared_kv and 2 * heads == V7X_SUBLANES
        ins.append(nb)
        in_specs.append(pl.BlockSpec((1, nb.shape[1], seq), lambda b, g, i: (b, 0, 0)))
    return pl.pallas_call(
        functools.partial(_attn_kernel, heads=heads, shared_kv=shared_kv, dk=dk, blk=blk,
                          has_bias=nb is not None),
        grid=(batch, groups, nq), in_specs=in_specs,
        out_specs=pl.BlockSpec((blk, heads * HEAD_DIM), lambda b, g, i: (b * nq + i, g)),
        out_shape=jax.ShapeDtypeStruct((batch * seq, groups * heads * HEAD_DIM), BF16),
        compiler_params=_params("parallel", "parallel", "arbitrary"),
        name="causal_attention_bias" if nb is not None else "causal_attention",
    )(*ins)


def _out_kernel(o_ref, z_ref, x_ref, w_ref, g_ref, y_ref):
    a = (o_ref[...].astype(F32) * _silu(z_ref[...].astype(F32))).astype(BF16)
    y_ref[...] = x_ref[...] + _rms(_dot(a, w_ref[...]), g_ref[...])


def _gated_out(o, z, x, w, g):
    rows, d = x.shape
    tm = min(ROW_TILE, rows)
    width = o.shape[1]
    return pl.pallas_call(
        _out_kernel, grid=(rows // tm,),
        in_specs=[pl.BlockSpec((tm, width), lambda i: (i, 0)), pl.BlockSpec((tm, width), lambda i: (i, 0)),
                  pl.BlockSpec((tm, d), lambda i: (i, 0)), _full(w.shape), _full((1, d))],
        out_specs=pl.BlockSpec((tm, d), lambda i: (i, 0)),
        out_shape=jax.ShapeDtypeStruct((rows, d), F32),
        compiler_params=_params("parallel"), name="gated_out",
    )(o, z, x, w, g.reshape(1, d))


def _decode_kernel(pt_ref, *refs, key_pos_minor, new_pos_minor, has_bias, pages, page, n_chunks):
    it = iter(refs)
    n_parts = len(key_pos_minor)
    n_pools = n_parts + (2 if has_bias else 0)
    q_refs = [next(it) for _ in range(n_parts)]
    knew_refs = [next(it) for _ in range(n_parts)]
    vnew_ref = next(it) if has_bias else None
    lfnew_ref = next(it) if has_bias else None
    pool_refs = [next(it) for _ in range(n_pools)]
    o_ref = next(it)
    bufs = [next(it) for _ in range(n_pools)]
    sem = next(it)
    b = pl.program_id(0)

    def copies(seq, chunk, slot):
        return [pltpu.make_async_copy(pool.at[pt_ref[seq, chunk * pages + j]], buf.at[slot, j],
                                      sem.at[slot, i])
                for i, (pool, buf) in enumerate(zip(pool_refs, bufs)) for j in range(pages)]

    @pl.when(b == 0)
    def _():
        for c in range(DEC_AHEAD):
            for cp in copies(0, c, c):
                cp.start()

    qs = [r[0] for r in q_refs]

    def scores(keys, pos_minor):
        return sum((_dot if pm else _dot_nt)(q, k) for q, k, pm in zip(qs, keys, pos_minor))

    def update(state, s, values, values_pos_minor):
        m, l, acc = state
        w = s.shape[1] // len(values)
        m_new = jnp.maximum(m, jnp.max(s, axis=1, keepdims=True))
        alpha = jnp.exp2(m - m_new)
        p = jnp.exp2(s - m_new)
        l = alpha * l + jnp.sum(p, axis=1, keepdims=True)
        pb = p.astype(BF16)
        pv = sum((_dot_nt if values_pos_minor else _dot)(pb[:, j * w:(j + 1) * w], vj)
                 for j, vj in enumerate(values))
        return m_new, l, alpha * acc + pv

    def grouped(buf, slot, pos_minor):
        return [jnp.concatenate([buf[slot, j + i] for i in range(DEC_PAGE_GROUP)],
                                axis=1 if pos_minor else 0).astype(BF16)
                for j in range(0, pages, DEC_PAGE_GROUP)]

    def tile_rows(nb):
        return jnp.concatenate([nb] * (DEC_ROWS // FOX_HEADS), axis=0)

    dv = bufs[n_parts].shape[2] if has_bias else bufs[0].shape[3]
    state = (jnp.full((DEC_ROWS, 1), -jnp.inf, F32), jnp.zeros((DEC_ROWS, 1), F32),
             jnp.zeros((DEC_ROWS, dv), F32))
    carry = jnp.zeros((FOX_HEADS, 1), F32)
    for c in range(n_chunks):
        slot = c
        for cp in copies(b, c, slot):
            cp.wait()
        ahead = c + DEC_AHEAD
        if ahead < n_chunks:
            for cp in copies(b, ahead, ahead):
                cp.start()
        else:
            @pl.when(b + 1 < pl.num_programs(0))
            def _():
                for cp in copies(b + 1, ahead - n_chunks, ahead - n_chunks):
                    cp.start()
        kb = [grouped(bufs[i], slot, key_pos_minor[i]) for i in range(n_parts)]
        s = jnp.concatenate([scores([kb[i][j] for i in range(n_parts)], key_pos_minor)
                             for j in range(pages // DEC_PAGE_GROUP)], axis=1)
        if has_bias:
            lf = jnp.concatenate([bufs[n_parts + 1][slot, j] for j in range(pages)], axis=1)
            cum, carry = _cumsum_lanes(lf, carry)
            s = s - tile_rows(LOG2E * cum)
            values = grouped(bufs[n_parts], slot, True)
        else:
            values = kb[0]
        state = update(state, s, values, has_bias)

    def pad(a):
        return jnp.concatenate([a, jnp.zeros((page - a.shape[0], a.shape[1]), a.dtype)], axis=0)

    knew = [(r[0] if pm else pad(r[0])).astype(BF16) for r, pm in zip(knew_refs, new_pos_minor)]
    sn = scores(knew, new_pos_minor)
    if has_bias:
        cum_new, _ = _cumsum_lanes(lfnew_ref[0], carry)
        sn = sn - tile_rows(LOG2E * cum_new)
        vnew = pad(vnew_ref[0]).astype(BF16)
    else:
        vnew = knew[0]
    tok = lax.broadcasted_iota(jnp.int32, sn.shape, 0) // FOX_HEADS
    new = lax.broadcasted_iota(jnp.int32, sn.shape, 1)
    sn = jnp.where(new <= tok, sn, -jnp.inf)
    _, l, acc = update(state, sn, [vnew], False)
    o = acc / l
    if has_bias:
        kvh = (lax.broadcasted_iota(jnp.int32, (DEC_ROWS, HEAD_DIM), 0) % FOX_HEADS) // FOX_GROUP
        o = sum(jnp.where(kvh == k, o[:, k * HEAD_DIM:(k + 1) * HEAD_DIM], 0.0)
                for k in range(FOX_KV_HEADS))
    o_ref[0] = o


def _decode_attention(page_table, qs, news, pools, key_pos_minor, new_pos_minor, *,
                      vnew=None, lfnew=None, v_pool=None, lf_pool=None):
    bd, n_pages = page_table.shape
    has_bias = lf_pool is not None
    pages = DEC_PAGES if n_pages % DEC_PAGES == 0 else 4
    page = pools[0].shape[2 if key_pos_minor[0] else 1]
    dv = v_pool.shape[1] if has_bias else pools[0].shape[2]

    n_chunks = n_pages // pages
    assert n_chunks >= 2 * DEC_AHEAD

    def seq_spec(a):
        return pl.BlockSpec((1,) + a.shape[1:], lambda b, pt: (b, 0, 0))

    ins = list(qs) + list(news)
    if has_bias:
        ins += [vnew, lfnew]
    in_specs = [seq_spec(a) for a in ins]
    all_pools = list(pools) + ([v_pool, lf_pool] if has_bias else [])
    ins += all_pools
    in_specs += [pl.BlockSpec(memory_space=pl.ANY)] * len(all_pools)
    scratch = [pltpu.VMEM((n_chunks, pages) + pool.shape[1:], F32) for pool in all_pools]
    scratch.append(pltpu.SemaphoreType.DMA((n_chunks, len(all_pools))))
    out_w = HEAD_DIM if has_bias else dv
    return pl.pallas_call(
        functools.partial(_decode_kernel, key_pos_minor=tuple(key_pos_minor),
                          new_pos_minor=tuple(new_pos_minor), has_bias=has_bias, pages=pages, page=page,
                          n_chunks=n_chunks),
        grid_spec=pltpu.PrefetchScalarGridSpec(
            num_scalar_prefetch=1, grid=(bd,), in_specs=in_specs,
            out_specs=pl.BlockSpec((1, DEC_ROWS, out_w), lambda b, pt: (b, 0, 0)),
            scratch_shapes=scratch),
        out_shape=jax.ShapeDtypeStruct((bd, DEC_ROWS, out_w), F32),
        compiler_params=_params("arbitrary"),
        name="decode_fox" if has_bias else "decode_mla",
    )(page_table, *ins)


def _pad_axis(a, axis, size):
    pad = [(0, 0)] * a.ndim
    pad[axis] = (0, size - a.shape[axis])
    return jnp.pad(a, pad)


def _conv_proj_kernel(x_ref, g_ref, wa_ref, wb_ref, wz_ref, u_ref, z_ref):
    h = _rms(x_ref[...], g_ref[...]).astype(BF16)
    u_ref[...] = _dot(h, wa_ref[...]) * jax.nn.sigmoid(_dot(h, wb_ref[...]))
    z_ref[...] = _dot(h, wz_ref[...]).astype(BF16)


def _conv_proj(x, g, w):
    rows, d = x.shape
    tm = min(ROW_TILE, rows)
    wa, wb, wz = w
    ch = wa.shape[1]
    return pl.pallas_call(
        _conv_proj_kernel, grid=(rows // tm,),
        in_specs=[pl.BlockSpec((tm, d), lambda i: (i, 0)), _full((1, d)), _full(wa.shape), _full(wb.shape),
                  _full(wz.shape)],
        out_specs=[pl.BlockSpec((tm, ch), lambda i: (i, 0)), pl.BlockSpec((tm, ch), lambda i: (i, 0))],
        out_shape=[jax.ShapeDtypeStruct((rows, ch), F32), jax.ShapeDtypeStruct((rows, ch), BF16)],
        compiler_params=_params("parallel"), name="conv_proj",
    )(x, g.reshape(1, d), wa, wb, wz)


def _ln_silu(c, g, b):
    cc = c - jnp.mean(c, axis=-1, keepdims=True)
    var = jnp.mean(cc * cc, axis=-1, keepdims=True)
    return _silu(cc * lax.rsqrt(var + NORM_EPS) * g + b)


def _conv_prompt_kernel(u_ref, halo_ref, dw_ref, dwb_ref, lng_ref, lnb_ref, c_ref, ext_ref, sh_ref, *, tm):
    first = pl.program_id(1) == 0
    ext_ref[0:CONV_HALO, :] = jnp.where(first, 0.0, halo_ref[0])
    ext_ref[CONV_HALO:, :] = u_ref[0]
    n_sh = tm + CONV_HALO - V7X_SUBLANES
    for sh in range(1, V7X_SUBLANES):
        sh_ref[sh - 1] = ext_ref[sh:sh + n_sh, :]
    base = CONV_HALO - (CONV_TAPS - 1)
    for r0 in range(0, tm, CONV_CHUNK):
        acc = jnp.zeros((CONV_CHUNK, u_ref.shape[2]), F32)
        for j in range(CONV_TAPS):
            tiles, sh = divmod(base + j, V7X_SUBLANES)
            src = ext_ref if sh == 0 else sh_ref.at[sh - 1]
            row = r0 + tiles * V7X_SUBLANES
            acc = acc + dw_ref[j:j + 1, :] * src[row:row + CONV_CHUNK, :]
        c = _ln_silu(acc + dwb_ref[...], lng_ref[...], lnb_ref[...])
        c_ref[r0:r0 + CONV_CHUNK, :] = c.astype(BF16)


def _conv_prompt(u, dw, dw_b, ln_g, ln_b):
    batch, seq, ch = u.shape
    tm = min(CONV_ROW_TILE, seq)
    nt = seq // tm
    hpt = tm // CONV_HALO
    return pl.pallas_call(
        functools.partial(_conv_prompt_kernel, tm=tm), grid=(batch, nt),
        in_specs=[pl.BlockSpec((1, tm, ch), lambda b, i: (b, i, 0)),
                  pl.BlockSpec((1, CONV_HALO, ch), lambda b, i: (b, jnp.maximum(i * hpt - 1, 0), 0)),
                  _full(dw.shape), _full((1, ch)), _full((1, ch)), _full((1, ch))],
        out_specs=pl.BlockSpec((tm, ch), lambda b, i: (b * nt + i, 0)),
        out_shape=jax.ShapeDtypeStruct((batch * seq, ch), BF16),
        scratch_shapes=[pltpu.VMEM((tm + CONV_HALO, ch), F32),
                        pltpu.VMEM((V7X_SUBLANES - 1, tm + CONV_HALO - V7X_SUBLANES, ch), F32)],
        compiler_params=_params("parallel", "arbitrary"), name="conv_prompt",
    )(u, u, dw, dw_b.reshape(1, ch), ln_g.reshape(1, ch), ln_b.reshape(1, ch))


def _conv_sample_kernel(st_ref, u_ref, dw_ref, dwb_ref, lng_ref, lnb_ref, c_ref, so_ref):
    n_state, t_new = st_ref.shape[0], u_ref.shape[0]

    def ext(i):
        return st_ref[i] if i < n_state else u_ref[i - n_state]

    for i in range(n_state):
        so_ref[i] = ext(i + t_new)
    for t in range(t_new):
        acc = dw_ref[0:1, :] * ext(t)
        for j in range(1, CONV_TAPS):
            acc = acc + dw_ref[j:j + 1, :] * ext(t + j)
        c_ref[t] = _ln_silu(acc + dwb_ref[...], lng_ref[...], lnb_ref[...])


def _conv_sample(state, u_new, dw, dw_b, ln_g, ln_b):
    n_state, bd, ch = state.shape
    t_new = u_new.shape[0]
    sb = min(CONV_SEQ_BLOCK, bd)
    return pl.pallas_call(
        _conv_sample_kernel, grid=(bd // sb,),
        in_specs=[pl.BlockSpec((n_state, sb, ch), lambda i: (0, i, 0)),
                  pl.BlockSpec((t_new, sb, ch), lambda i: (0, i, 0)),
                  _full(dw.shape), _full((1, ch)), _full((1, ch)), _full((1, ch))],
        out_specs=[pl.BlockSpec((t_new, sb, ch), lambda i: (0, i, 0)),
                   pl.BlockSpec((n_state, sb, ch), lambda i: (0, i, 0))],
        out_shape=[jax.ShapeDtypeStruct((t_new, bd, ch), F32), jax.ShapeDtypeStruct((n_state, bd, ch), F32)],
        compiler_params=_params("parallel"), name="conv_sample",
    )(state, u_new, dw, dw_b.reshape(1, ch), ln_g.reshape(1, ch), ln_b.reshape(1, ch))


def _rope_lane_group(x, c, sa, sb):
    return x * c + pltpu.roll(x, V7X_LANES - MLA_ROPE // 2, 1) * sa + pltpu.roll(x, MLA_ROPE // 2, 1) * sb


def _mla_proj_kernel(x_ref, g_ref, wcq_ref, wckv_ref, wkpe_ref, wz_ref, qn_ref, kvn_ref, wuq_ref,
                     c_ref, sa_ref, sb_ref, *rest, prompt):
    if prompt:
        wukn_ref, wuv_ref, q_ref, ckv_ref, kpe_ref, z_ref, kh_ref, vh_ref = rest
    else:
        q_ref, ckv_ref, kpe_ref, z_ref = rest
    scale = LOG2E * (HEAD_DIM + MLA_ROPE) ** -0.5
    c, sa, sb = c_ref[...], sa_ref[...], sb_ref[...]
    h = _rms(x_ref[...], g_ref[...]).astype(BF16)
    cq = _rms(_dot(h, wcq_ref[...]), qn_ref[...]).astype(BF16)
    ckv = _rms(_dot(h, wckv_ref[...]), kvn_ref[...])
    ckv_ref[...] = ckv
    kp = _rope_lane_group(_dot(h, wkpe_ref[...]), c, sa, sb)
    kpe_ref[...] = kp[:, HEAD_DIM:HEAD_DIM + MLA_ROPE]
    z_ref[...] = _dot(h, wz_ref[...]).astype(BF16)
    q = _dot(cq, wuq_ref[...])
    for hd in range(MLA_HEADS):
        sl = slice(hd * MLA_QK_PAD, (hd + 1) * MLA_QK_PAD)
        q_ref[:, sl] = (_rope_lane_group(q[:, sl], c, sa, sb) * scale).astype(BF16)
    if prompt:
        ckv_b = ckv.astype(BF16)
        kn = _dot(ckv_b, wukn_ref[...])
        v = _dot(ckv_b, wuv_ref[...])
        for hd in range(MLA_HEADS):
            kh_ref[0, hd] = (kn[:, hd * MLA_QK_PAD:(hd + 1) * MLA_QK_PAD] + kp).astype(BF16)
            vh_ref[0, hd] = _values_with_ones(v, hd)


def _mla_proj(x, g, w, q_norm, kv_norm, tables, *, batch=None):
    rows, d = x.shape
    prompt = batch is not None
    tm = min(ROW_TILE // 2, rows)
    nt = rows // tm
    wcq, wckv, wkpe, wz, wuq, wukn, wuv = w
    tab_rows = tables[0].shape[0]
    tpt = tab_rows // tm
    ins = [x, g.reshape(1, d), wcq, wckv, wkpe, wz, q_norm.reshape(1, -1), kv_norm.reshape(1, -1), wuq,
           *tables]
    in_specs = [pl.BlockSpec((tm, d), lambda i: (i, 0)), _full((1, d)), _full(wcq.shape), _full(wckv.shape),
                _full(wkpe.shape), _full(wz.shape), _full((1, wcq.shape[1])), _full((1, wckv.shape[1])),
                _full(wuq.shape)]
    in_specs += [pl.BlockSpec((tm, V7X_LANES), lambda i: (i % tpt, 0)) for _ in tables]
    qw = wuq.shape[1]
    out_shape = [jax.ShapeDtypeStruct((rows, qw), BF16), jax.ShapeDtypeStruct((rows, MLA_KV_LORA), F32),
                 jax.ShapeDtypeStruct((rows, MLA_ROPE), F32), jax.ShapeDtypeStruct((rows, wz.shape[1]), BF16)]
    out_specs = [pl.BlockSpec((tm, qw), lambda i: (i, 0)), pl.BlockSpec((tm, MLA_KV_LORA), lambda i: (i, 0)),
                 pl.BlockSpec((tm, MLA_ROPE), lambda i: (i, 0)), pl.BlockSpec((tm, wz.shape[1]), lambda i: (i, 0))]
    if prompt:
        seq = rows // batch
        tps = seq // tm
        ins += [wukn, wuv]
        in_specs += [_full(wukn.shape), _full(wuv.shape)]
        out_shape += [jax.ShapeDtypeStruct((batch, MLA_HEADS, seq, MLA_QK_PAD), BF16),
                      jax.ShapeDtypeStruct((batch, MLA_HEADS, seq, V7X_LANES), BF16)]
        out_specs += [pl.BlockSpec((1, MLA_HEADS, tm, MLA_QK_PAD), lambda i: (i // tps, 0, i % tps, 0)),
                      pl.BlockSpec((1, MLA_HEADS, tm, V7X_LANES), lambda i: (i // tps, 0, i % tps, 0))]
    return pl.pallas_call(
        functools.partial(_mla_proj_kernel, prompt=prompt), grid=(nt,), in_specs=in_specs,
        out_specs=out_specs, out_shape=out_shape, compiler_params=_params("parallel"),
        name="mla_proj_prompt" if prompt else "mla_proj_sample",
    )(*ins)


def _rope_tables(positions):
    half = MLA_ROPE // 2
    inv = ROPE_THETA ** (-np.arange(half, dtype=np.float32) / half)
    ang = np.asarray(positions, np.float32)[:, None] * inv[None, :]
    cos, sin = np.cos(ang).astype(np.float32), np.sin(ang).astype(np.float32)
    n = ang.shape[0]
    c = np.zeros((n, V7X_LANES), np.float32)
    sa = np.zeros((n, V7X_LANES), np.float32)
    sb = np.zeros((n, V7X_LANES), np.float32)
    c[:, :HEAD_DIM] = 1.0
    c[:, HEAD_DIM:HEAD_DIM + half] = cos
    c[:, HEAD_DIM + half:HEAD_DIM + 2 * half] = cos
    sa[:, HEAD_DIM:HEAD_DIM + half] = -sin
    sb[:, HEAD_DIM + half:HEAD_DIM + 2 * half] = sin
    return jnp.asarray(c), jnp.asarray(sa), jnp.asarray(sb)


def _headwise_kernel(x_ref, w_ref, o_ref):
    o_ref[0] = _dot(x_ref[0], w_ref[0]).astype(o_ref.dtype)


def _headwise_matmul(x, w, dtype):
    nh, rows, k = x.shape
    n = w.shape[2]
    return pl.pallas_call(
        _headwise_kernel, grid=(nh,),
        in_specs=[pl.BlockSpec((1, rows, k), lambda i: (i, 0, 0)), pl.BlockSpec((1, k, n), lambda i: (i, 0, 0))],
        out_specs=pl.BlockSpec((1, rows, n), lambda i: (i, 0, 0)),
        out_shape=jax.ShapeDtypeStruct((nh, rows, n), dtype),
        compiler_params=_params("parallel"), name="headwise_matmul",
    )(x, w)


def _fox_weights(w_in):
    d = w_in.shape[0]
    width = FOX_HEADS * HEAD_DIM
    kvw = FOX_KV_HEADS * HEAD_DIM
    o1, o2, o3 = width + kvw, width + 2 * kvw, 2 * width + 2 * kvw
    wb = w_in.astype(BF16)
    wf = jnp.zeros((d, V7X_LANES), BF16).at[:, :FOX_HEADS].set(wb[:, o3:])
    return wb[:, :width], wb[:, width:o1], wb[:, o1:o2], wb[:, o2:o3], wf, wb[:, o3:].T


def _fox_layer(xp, xs, cache_k, cache_v, cache_logf, page_table, g_pre, w_in, b_f, w_out, g_post, dims):
    batch, seq, bd, t_new = dims
    w = _fox_weights(w_in)
    w_out = w_out.astype(BF16)
    n_pool, page = cache_k.shape[:2]
    kvw = FOX_KV_HEADS * HEAD_DIM

    q, k, v, z, lft, kh, vh, nb = _fox_proj(xp, g_pre, w, b_f, batch=batch)
    o = _causal_attention(q, kh, vh, nb, heads=FOX_GROUP, shared_kv=True, dk=HEAD_DIM)
    xp = _gated_out(o, z, xp, w_out, g_post)

    qs, ks, vs, zs, lfs = _fox_proj(xs, g_pre, w, b_f)
    own = (np.arange(FOX_HEADS)[:, None] // FOX_GROUP == np.arange(FOX_KV_HEADS)[None, :])
    qbd = (qs.reshape(bd, t_new, FOX_HEADS, 1, HEAD_DIM)
           * jnp.asarray(own, BF16)[None, None, :, :, None]).reshape(bd, t_new * FOX_HEADS, kvw)
    k_pool = cache_k.transpose(0, 2, 3, 1).reshape(n_pool, kvw, page)
    v_pool = cache_v.transpose(0, 2, 3, 1).reshape(n_pool, kvw, page)
    lf_pool = cache_logf.transpose(0, 2, 1)
    os_ = _decode_attention(
        page_table, [qbd], [_pad_axis(ks.reshape(bd, t_new, kvw), 1, V7X_SUBLANES)], [k_pool], (True,), (False,),
        vnew=_pad_axis(vs.reshape(bd, t_new, kvw), 1, V7X_SUBLANES),
        lfnew=_pad_axis(lfs.reshape(bd, t_new, FOX_HEADS).transpose(0, 2, 1), 2, page),
        v_pool=v_pool, lf_pool=lf_pool)
    xs = _gated_out(os_.reshape(bd * t_new, FOX_HEADS * HEAD_DIM), zs, xs, w_out, g_post)

    new_p = (k.reshape(batch, seq, FOX_KV_HEADS, HEAD_DIM), v.reshape(batch, seq, FOX_KV_HEADS, HEAD_DIM),
             lft.transpose(0, 2, 1))
    new_s = (ks.reshape(bd, t_new, FOX_KV_HEADS, HEAD_DIM), vs.reshape(bd, t_new, FOX_KV_HEADS, HEAD_DIM),
             lfs.reshape(bd, t_new, FOX_HEADS))
    return xp, xs, new_p, new_s


def _conv_layer(xp, xs, state, g_pre, w_in, dw, dw_b, ln_g, ln_b, w_out, g_post, dims):
    batch, seq, bd, t_new = dims
    ch = dw.shape[1]
    wb = w_in.astype(BF16)
    w = (wb[:, :ch], wb[:, ch:2 * ch], wb[:, 2 * ch:])
    w_out = w_out.astype(BF16)

    u, z = _conv_proj(xp, g_pre, w)
    u3 = u.reshape(batch, seq, ch)
    c = _conv_prompt(u3, dw, dw_b, ln_g, ln_b)
    xp = _gated_out(c, z, xp, w_out, g_post)

    us, zs = _conv_proj(xs, g_pre, w)
    cs, state_new = _conv_sample(state.transpose(1, 0, 2), us.reshape(bd, t_new, ch).transpose(1, 0, 2),
                                 dw, dw_b, ln_g, ln_b)
    xs = _gated_out(cs.transpose(1, 0, 2).reshape(bd * t_new, ch), zs, xs, w_out, g_post)
    return xp, xs, (u3[:, seq - (CONV_TAPS - 1):],), (state_new.transpose(1, 0, 2),)


def _mla_layer(xp, xs, cache_ckv, cache_kpe, page_table, g_pre, w_in, q_norm, w_uq, kv_norm, w_ukv, w_out,
               g_post, dims):
    batch, seq, bd, t_new = dims
    d = w_in.shape[0]
    page = cache_ckv.shape[1]
    past = page_table.shape[1] * page
    o0, o1, o2 = MLA_Q_LORA, MLA_Q_LORA + MLA_KV_LORA, MLA_Q_LORA + MLA_KV_LORA + MLA_ROPE
    wb = w_in.astype(BF16)
    wkpe = jnp.zeros((d, V7X_LANES), BF16).at[:, HEAD_DIM:HEAD_DIM + MLA_ROPE].set(wb[:, o1:o2])
    dqk = HEAD_DIM + MLA_ROPE
    wuq = jnp.pad(w_uq.astype(BF16).reshape(MLA_Q_LORA, MLA_HEADS, dqk),
                  ((0, 0), (0, 0), (0, MLA_QK_PAD - dqk))).reshape(MLA_Q_LORA, MLA_HEADS * MLA_QK_PAD)
    wukv = w_ukv.astype(BF16).reshape(MLA_KV_LORA, MLA_HEADS, 2 * HEAD_DIM)
    wuk, wuv = wukv[..., :HEAD_DIM], wukv[..., HEAD_DIM:]
    wukn = jnp.pad(wuk, ((0, 0), (0, 0), (0, MLA_QK_PAD - HEAD_DIM))).reshape(MLA_KV_LORA, MLA_HEADS * MLA_QK_PAD)
    w = (wb[:, :o0], wb[:, o0:o1], wkpe, wb[:, o2:], wuq, wukn, wuv.reshape(MLA_KV_LORA, MLA_HEADS * HEAD_DIM))
    w_out = w_out.astype(BF16)

    q, ckv, kpe, z, kh, vh = _mla_proj(xp, g_pre, w, q_norm, kv_norm, _rope_tables(np.arange(seq)), batch=batch)
    o = _causal_attention(q, kh, vh, None, heads=2, shared_kv=False, dk=MLA_QK_PAD)
    xp = _gated_out(o, z, xp, w_out, g_post)

    rows = bd * t_new
    pos_s = np.tile(past + np.arange(t_new), bd)
    qs, ckvs, kpes, zs = _mla_proj(xs, g_pre, w, q_norm, kv_norm, _rope_tables(pos_s))
    q4 = qs.reshape(rows, MLA_HEADS, MLA_QK_PAD)
    q_lat = _headwise_matmul(q4[..., :HEAD_DIM].transpose(1, 0, 2), wuk.transpose(1, 2, 0), BF16)
    q_lat = q_lat.transpose(1, 0, 2).reshape(bd, t_new * MLA_HEADS, MLA_KV_LORA)
    q_pe = q4[..., HEAD_DIM:HEAD_DIM + MLA_ROPE].reshape(bd, t_new * MLA_HEADS, MLA_ROPE)
    o_lat = _decode_attention(
        page_table, [q_lat, q_pe],
        [_pad_axis(ckvs.reshape(bd, t_new, MLA_KV_LORA), 1, V7X_SUBLANES),
         _pad_axis(kpes.reshape(bd, t_new, MLA_ROPE).transpose(0, 2, 1), 2, page)],
        [cache_ckv, cache_kpe.transpose(0, 2, 1)], (False, True), (False, True))
    o_lat = o_lat.astype(BF16).reshape(rows, MLA_HEADS, MLA_KV_LORA).transpose(1, 0, 2)
    os_ = _headwise_matmul(o_lat, wuv.transpose(1, 0, 2), BF16).transpose(1, 0, 2)
    xs = _gated_out(os_.reshape(rows, MLA_HEADS * HEAD_DIM), zs, xs, w_out, g_post)

    new_p = (ckv.reshape(batch, seq, MLA_KV_LORA), kpe.reshape(batch, seq, MLA_ROPE))
    new_s = (ckvs.reshape(bd, t_new, MLA_KV_LORA), kpes.reshape(bd, t_new, MLA_ROPE))
    return xp, xs, new_p, new_s


def kernel(x_prompt, x_sample, cache_k_l0, cache_v_l0, cache_logf_l0, state_conv_l1, cache_ckv_l2, cache_kpe_l2, cache_k_l3, cache_v_l3, cache_logf_l3, page_table, norm_pre_l0, w_in_l0, b_f_l0, w_out_l0, norm_post_l0, norm_pre_l1, w_in_l1, dw_l1, dw_b_l1, ln_g_l1, ln_b_l1, w_out_l1, norm_post_l1, norm_pre_l2, w_in_l2, q_norm_l2, w_uq_l2, kv_norm_l2, w_ukv_l2, w_out_l2, norm_post_l2, norm_pre_l3, w_in_l3, b_f_l3, w_out_l3, norm_post_l3):
    batch, seq, d = x_prompt.shape
    bd, t_new, _ = x_sample.shape
    assert t_new * FOX_HEADS == DEC_ROWS and t_new * MLA_HEADS == DEC_ROWS
    dims = (batch, seq, bd, t_new)
    xp = x_prompt.reshape(batch * seq, d)
    xs = x_sample.reshape(bd * t_new, d)

    xp, xs, (k_p0, v_p0, lf_p0), (k_s0, v_s0, lf_s0) = _fox_layer(
        xp, xs, cache_k_l0, cache_v_l0, cache_logf_l0, page_table,
        norm_pre_l0, w_in_l0, b_f_l0, w_out_l0, norm_post_l0, dims)
    xp, xs, (conv_p1,), (conv_s1,) = _conv_layer(
        xp, xs, state_conv_l1, norm_pre_l1, w_in_l1, dw_l1, dw_b_l1, ln_g_l1, ln_b_l1, w_out_l1,
        norm_post_l1, dims)
    xp, xs, (ckv_p2, kpe_p2), (ckv_s2, kpe_s2) = _mla_layer(
        xp, xs, cache_ckv_l2, cache_kpe_l2, page_table,
        norm_pre_l2, w_in_l2, q_norm_l2, w_uq_l2, kv_norm_l2, w_ukv_l2, w_out_l2, norm_post_l2, dims)
    xp, xs, (k_p3, v_p3, lf_p3), (k_s3, v_s3, lf_s3) = _fox_layer(
        xp, xs, cache_k_l3, cache_v_l3, cache_logf_l3, page_table,
        norm_pre_l3, w_in_l3, b_f_l3, w_out_l3, norm_post_l3, dims)

    return (xp.reshape(batch, seq, d), xs.reshape(bd, t_new, d),
            k_p0, v_p0, lf_p0, k_s0, v_s0, lf_s0,
            conv_p1, conv_s1,
            ckv_p2, kpe_p2, ckv_s2, kpe_s2,
            k_p3, v_p3, lf_p3, k_s3, v_s3, lf_s3)
```
